```python
import jax, jax.numpy as jnp
from jax import lax
import numpy as np

D_MODEL = 1024
BATCH = 8
SEQ = 2048
DEPTH = 4
DEC_BATCH = 128
DEC_SEQ = 4
PAST_LEN = 16384
PAGE_SIZE = 128

N_META = 16
MIX_WIDTH = D_MODEL
DN_WIDTH = MIX_WIDTH // 2
RW_WIDTH = MIX_WIDTH - DN_WIDTH
DN_HEADS = 4
DN_HEAD_DIM = DN_WIDTH // DN_HEADS
RW_HEAD_DIM = 64
RW_HEADS = RW_WIDTH // RW_HEAD_DIM
CONV_W = 4
CHUNK = 64
W_LORA = 64
A_LORA = 64
G_LORA = 128
D_FF = -(-8 * D_MODEL // (3 * 256)) * 256
DN_QKV = 3 * DN_WIDTH
DN_COLS = DN_QKV + DN_WIDTH + 2 * DN_HEADS
RW_COLS = 3 * RW_WIDTH + W_LORA + A_LORA + G_LORA
IN_COLS = DN_COLS + RW_COLS
RMS_EPS = 1e-6
L2_EPS = 1e-6
GN_EPS = 64e-5
F32 = jnp.float32

kernel_name = 'hybrid_gdn_rwkv7_decoder_step'


def _rmsnorm(x, g):
    xf = x.astype(F32)
    return xf * lax.rsqrt(jnp.mean(xf * xf, axis=-1, keepdims=True) + RMS_EPS) * g.astype(F32)


def _l2norm(x):
    return x * lax.rsqrt(jnp.sum(x * x, axis=-1, keepdims=True) + L2_EPS)


def _causal_conv(x, buf, w):
    L = x.shape[1]
    xp = jnp.concatenate([buf.astype(x.dtype), x], axis=1)
    y = sum(xp[:, i:i + L].astype(F32) * w[i].astype(F32) for i in range(CONV_W))
    return jax.nn.silu(y), xp[:, L:]


def _gdn_chunked(q, k, v, glog, beta, s0, chunk):
    B, L, H, Dk = q.shape
    Dv = v.shape[-1]
    n = L // chunk

    def blk(t):
        return jnp.moveaxis(t.reshape((B, n, chunk, H) + t.shape[3:]), 3, 2)

    qc, kc, vc, gc, bc = blk(q), blk(k), blk(v), blk(glog), blk(beta)
    gcum = jnp.cumsum(gc, axis=-1)
    tri = jnp.tril(jnp.ones((chunk, chunk), bool))
    strict = jnp.tril(jnp.ones((chunk, chunk), bool), -1)
    decay = jnp.exp(jnp.where(tri, gcum[..., :, None] - gcum[..., None, :], -jnp.inf))
    kb = kc * bc[..., None]
    a_mat = jnp.where(strict, jnp.einsum('bnhcd,bnhsd->bnhcs', kb, kc) * decay, 0.0)
    eye = jnp.eye(chunk, dtype=F32)
    t_mat = lax.linalg.triangular_solve(a_mat + eye, jnp.broadcast_to(eye, a_mat.shape),
                                        left_side=True, lower=True, unit_diagonal=True)
    u = jnp.einsum('bnhcs,bnhsv->bnhcv', t_mat, vc * bc[..., None])
    w = jnp.einsum('bnhcs,bnhsk->bnhck', t_mat, kb * jnp.exp(gcum)[..., None])
    qk = jnp.einsum('bnhcd,bnhsd->bnhcs', qc, kc) * decay

    def step(s, xs):
        qi, ki, ui, wi, gi, ai = xs
        v_new = ui - jnp.einsum('bhck,bhkv->bhcv', wi, s)
        o = (jnp.einsum('bhck,bhkv->bhcv', qi * jnp.exp(gi)[..., None], s)
             + jnp.einsum('bhcs,bhsv->bhcv', ai, v_new))
        g_last = gi[..., -1]
        k_dec = ki * jnp.exp(g_last[..., None] - gi)[..., None]
        s = s * jnp.exp(g_last)[..., None, None] + jnp.einsum('bhck,bhcv->bhkv', k_dec, v_new)
        return s, o

    xs = tuple(jnp.moveaxis(t, 1, 0) for t in (qc, kc, u, w, gcum, qk))
    s, o = lax.scan(step, s0, xs)
    o = jnp.transpose(o, (1, 0, 3, 2, 4)).reshape(B, L, H, Dv)
    return o, s


def _rwkv7_scan(r, w, k, v, kk, a, s0):
    def step(s, xs):
        rt, wt, kt, vt, kkt, at = xs
        sa = jnp.einsum('bhij,bhj->bhi', s, -kkt)
        s = (s * wt[:, :, None, :] + sa[..., None] * (kkt * at)[:, :, None, :]
             + vt[..., None] * kt[:, :, None, :])
        return s, jnp.einsum('bhij,bhj->bhi', s, rt)

    xs = tuple(jnp.swapaxes(t, 0, 1) for t in (r, w, k, v, kk, a))
    s, y = lax.scan(step, s0, xs)
    return jnp.swapaxes(y, 0, 1), s


def _layer(x, l, P, conv_buf, dn_s, shift_buf, rw_s, n_lead):
    dt = x.dtype
    B, L, _ = x.shape
    h = _rmsnorm(x, P['g_pre_mix'][l]).astype(dt)
    proj = h @ P['w_in'][l]

    qkv, conv_new = _causal_conv(proj[..., :DN_QKV], conv_buf, P['dn_conv_w'][l])
    dheads = lambda t: t.reshape(B, L, DN_HEADS, DN_HEAD_DIM)
    q = _l2norm(dheads(qkv[..., :DN_WIDTH])) * DN_HEAD_DIM ** -0.5
    k = _l2norm(dheads(qkv[..., DN_WIDTH:2 * DN_WIDTH]))
    v = dheads(qkv[..., 2 * DN_WIDTH:])
    z = dheads(proj[..., DN_QKV:DN_QKV + DN_WIDTH].astype(F32))
    off = DN_QKV + DN_WIDTH
    beta = jax.nn.sigmoid(proj[..., off:off + DN_HEADS].astype(F32))
    glog = -jnp.exp(P['dn_a_log'][l].astype(F32)) * jax.nn.softplus(
        proj[..., off + DN_HEADS:DN_COLS].astype(F32) + P['dn_dt_bias'][l].astype(F32))
    s0 = dn_s.astype(F32)
    if n_lead:
        o_a, s1 = _gdn_chunked(q[:, :n_lead], k[:, :n_lead], v[:, :n_lead],
                               glog[:, :n_lead], beta[:, :n_lead], s0, n_lead)
        o_b, dn_new = _gdn_chunked(q[:, n_lead:], k[:, n_lead:], v[:, n_lead:],
                                   glog[:, n_lead:], beta[:, n_lead:], s1, CHUNK)
        o_dn = jnp.concatenate([o_a, o_b], axis=1)
    else:
        o_dn, dn_new = _gdn_chunked(q, k, v, glog, beta, s0, L)
    o_dn = (_rmsnorm(o_dn, P['dn_norm_w'][l]) * jax.nn.silu(z)).reshape(B, L, DN_WIDTH)

    rheads = lambda t: t.reshape(B, L, RW_HEADS, RW_HEAD_DIM)
    cols = proj[..., DN_COLS:]
    shift_new = cols[:, -1]
    cf = cols.astype(F32)
    prev = jnp.concatenate([shift_buf[:, None].astype(F32), cf[:, :-1]], axis=1)
    xs = cf + (prev - cf) * P['rw_mu'][l].astype(F32)
    r = xs[..., :RW_WIDTH]
    kr = xs[..., RW_WIDTH:2 * RW_WIDTH]
    vr = xs[..., 2 * RW_WIDTH:3 * RW_WIDTH]
    off = 3 * RW_WIDTH
    wd = xs[..., off:off + W_LORA]
    ad = xs[..., off + W_LORA:off + W_LORA + A_LORA]
    gd = xs[..., off + W_LORA + A_LORA:]
    w_log = -jax.nn.softplus(-(P['rw_w0'][l].astype(F32) + jnp.tanh(wd) @ P['rw_w_up'][l].astype(F32))) - 0.5
    decay = jnp.exp(-jnp.exp(w_log))
    a = jax.nn.sigmoid(P['rw_a0'][l].astype(F32) + ad @ P['rw_a_up'][l].astype(F32))
    g = jax.nn.sigmoid(gd) @ P['rw_g_up'][l].astype(F32)
    kk = _l2norm(rheads(kr * P['rw_k_k'][l].astype(F32)))
    kr = kr * (1.0 + (a - 1.0) * P['rw_k_a'][l].astype(F32))
    y, rw_new = _rwkv7_scan(rheads(r), rheads(decay), rheads(kr), rheads(vr), kk, rheads(a),
                            rw_s.astype(F32))
    mu = jnp.mean(y, axis=-1, keepdims=True)
    var = jnp.mean(jnp.square(y - mu), axis=-1, keepdims=True)
    y = ((y - mu) * lax.rsqrt(var + GN_EPS)).reshape(B, L, RW_WIDTH) * P['rw_ln_w'][l].astype(F32) \
        + P['rw_ln_b'][l].astype(F32)
    bonus = jnp.sum(rheads(r) * rheads(kr) * P['rw_r_k'][l].astype(F32), axis=-1, keepdims=True) * rheads(vr)
    o_rw = (y + bonus.reshape(B, L, RW_WIDTH)) * g

    mixed = jnp.concatenate([o_dn, o_rw], axis=-1).astype(dt) @ P['w_out'][l]
    x = x + _rmsnorm(mixed, P['g_post_mix'][l]).astype(dt)

    h = _rmsnorm(x, P['g_pre_ffn'][l]).astype(dt)
    f = (jax.nn.silu(h @ P['w_gate'][l]) * (h @ P['w_up'][l])) @ P['w_down'][l]
    x = x + _rmsnorm(f, P['g_post_ffn'][l]).astype(dt)
    return x, (conv_new.astype(dt), dn_new.astype(dt), shift_new.astype(dt), rw_new.astype(dt))


def _trunk(x, states, P, n_lead):
    new = ([], [], [], [])
    for l in range(DEPTH):
        x, st = _layer(x, l, P, states[0][l], states[1][l], states[2][l], states[3][l], n_lead)
        for lst, s in zip(new, st):
            lst.append(s)
    return x, (jnp.stack(new[0]), jnp.stack(new[1]), jnp.stack(new[2]), jnp.stack(new[3]))


def setup_inputs(seed: int = 0) -> dict:
    key = jax.random.key(seed)
    ks = jax.random.split(key, 40)
    nrm = lambda i, shape, s: jax.random.normal(ks[i], shape, F32) * s
    gain = lambda i, shape: 1.0 + 0.01 * jax.random.normal(ks[i], shape, F32)
    return {
        'x_prompt': nrm(0, (BATCH, SEQ, D_MODEL), 1.0),
        'x_sample': nrm(1, (DEC_BATCH, DEC_SEQ, D_MODEL), 1.0),
        'state_dn_conv': nrm(2, (DEPTH, DEC_BATCH, CONV_W - 1, DN_QKV), 1.0),
        'state_dn': nrm(3, (DEPTH, DEC_BATCH, DN_HEADS, DN_HEAD_DIM, DN_HEAD_DIM), 0.3),
        'state_rw_shift': nrm(4, (DEPTH, DEC_BATCH, RW_COLS), 1.0),
        'state_rw': nrm(5, (DEPTH, DEC_BATCH, RW_HEADS, RW_HEAD_DIM, RW_HEAD_DIM), 0.3),
        'meta': nrm(6, (N_META, D_MODEL), 1.0),
        'g_pre_mix': gain(7, (DEPTH, D_MODEL)),
        'g_post_mix': gain(8, (DEPTH, D_MODEL)),
        'g_pre_ffn': gain(9, (DEPTH, D_MODEL)),
        'g_post_ffn': gain(10, (DEPTH, D_MODEL)),
        'w_in': nrm(11, (DEPTH, D_MODEL, IN_COLS), D_MODEL ** -0.5),
        'dn_conv_w': nrm(12, (DEPTH, CONV_W, DN_QKV), 0.5),
        'dn_a_log': jnp.log(jax.random.uniform(ks[13], (DEPTH, DN_HEADS), F32, 1.0, 16.0)),
        'dn_dt_bias': nrm(14, (DEPTH, DN_HEADS), 0.1) - 2.0,
        'dn_norm_w': gain(15, (DEPTH, DN_HEAD_DIM)),
        'rw_mu': jax.random.uniform(ks[16], (DEPTH, RW_COLS), F32),
        'rw_w0': nrm(17, (DEPTH, RW_WIDTH), 0.5),
        'rw_w_up': nrm(18, (DEPTH, W_LORA, RW_WIDTH), W_LORA ** -0.5),
        'rw_a0': nrm(19, (DEPTH, RW_WIDTH), 0.1),
        'rw_a_up': nrm(20, (DEPTH, A_LORA, RW_WIDTH), A_LORA ** -0.5),
        'rw_g_up': nrm(21, (DEPTH, G_LORA, RW_WIDTH), G_LORA ** -0.5),
        'rw_k_k': 0.85 + nrm(22, (DEPTH, RW_WIDTH), 0.05),
        'rw_k_a': 1.0 + nrm(23, (DEPTH, RW_WIDTH), 0.05),
        'rw_r_k': nrm(24, (DEPTH, RW_HEADS, RW_HEAD_DIM), 0.1),
        'rw_ln_w': gain(25, (DEPTH, RW_WIDTH)),
        'rw_ln_b': nrm(26, (DEPTH, RW_WIDTH), 0.01),
        'w_out': nrm(27, (DEPTH, MIX_WIDTH, D_MODEL), MIX_WIDTH ** -0.5),
        'w_gate': nrm(28, (DEPTH, D_MODEL, D_FF), D_MODEL ** -0.5),
        'w_up': nrm(29, (DEPTH, D_MODEL, D_FF), D_MODEL ** -0.5),
        'w_down': nrm(30, (DEPTH, D_FF, D_MODEL), D_FF ** -0.5),
    }


def reference(x_prompt, x_sample, state_dn_conv, state_dn, state_rw_shift, state_rw, meta,
              g_pre_mix, g_post_mix, g_pre_ffn, g_post_ffn, w_in, dn_conv_w, dn_a_log,
              dn_dt_bias, dn_norm_w, rw_mu, rw_w0, rw_w_up, rw_a0, rw_a_up, rw_g_up,
              rw_k_k, rw_k_a, rw_r_k, rw_ln_w, rw_ln_b, w_out, w_gate, w_up, w_down):
    P = {'g_pre_mix': g_pre_mix, 'g_post_mix': g_post_mix, 'g_pre_ffn': g_pre_ffn,
         'g_post_ffn': g_post_ffn, 'w_in': w_in, 'dn_conv_w': dn_conv_w, 'dn_a_log': dn_a_log,
         'dn_dt_bias': dn_dt_bias, 'dn_norm_w': dn_norm_w, 'rw_mu': rw_mu, 'rw_w0': rw_w0,
         'rw_w_up': rw_w_up, 'rw_a0': rw_a0, 'rw_a_up': rw_a_up, 'rw_g_up': rw_g_up,
         'rw_k_k': rw_k_k, 'rw_k_a': rw_k_a, 'rw_r_k': rw_r_k, 'rw_ln_w': rw_ln_w,
         'rw_ln_b': rw_ln_b, 'w_out': w_out, 'w_gate': w_gate, 'w_up': w_up, 'w_down': w_down}
    dt = x_prompt.dtype
    B = x_prompt.shape[0]
    xp = jnp.concatenate([jnp.broadcast_to(meta.astype(dt)[None], (B, N_META, D_MODEL)), x_prompt], axis=1)
    zero_states = (jnp.zeros((DEPTH, B, CONV_W - 1, DN_QKV), dt),
                   jnp.zeros((DEPTH, B, DN_HEADS, DN_HEAD_DIM, DN_HEAD_DIM), dt),
                   jnp.zeros((DEPTH, B, RW_COLS), dt),
                   jnp.zeros((DEPTH, B, RW_HEADS, RW_HEAD_DIM, RW_HEAD_DIM), dt))
    yp, (cp, dp, sp, rp) = _trunk(xp, zero_states, P, N_META)
    ys, (cs, ds, ss, rs) = _trunk(x_sample, (state_dn_conv, state_dn, state_rw_shift, state_rw), P, 0)
    return (yp[:, N_META:], ys, cp, dp, sp, rp, cs, ds, ss, rs)
```

```python
import functools
import math

import jax
import jax.numpy as jnp
from jax import lax
from jax.experimental import pallas as pl
from jax.experimental.pallas import tpu as pltpu

F32 = jnp.float32
BF16 = jnp.bfloat16
HI = lax.Precision.HIGHEST

D_MODEL = 1024
DEPTH = 4
N_META = 16
DN_HEADS = 4
DN_HEAD_DIM = 128
DN_WIDTH = DN_HEADS * DN_HEAD_DIM
DN_QKV = 3 * DN_WIDTH
RW_HEADS = 8
RW_HEAD_DIM = 64
RW_WIDTH = RW_HEADS * RW_HEAD_DIM
W_LORA = 64
A_LORA = 64
G_LORA = 128
RW_COLS = 3 * RW_WIDTH + W_LORA + A_LORA + G_LORA
D_FF = 2816
CONV_W = 4
RMS_EPS = 1e-6
L2_EPS = 1e-6
GN_EPS = 64e-5

LANES = 128
SUBLANES = 8
VMEM_LIMIT = 56 * 1024 * 1024

C_QKV = 0
C_Z = DN_QKV
C_BG = C_Z + DN_WIDTH
C_RW = C_BG + LANES
P_COLS = C_RW + RW_COLS
PROMPT_CHUNK = 64


def _sigmoid(x):
    return 1.0 / (1.0 + jnp.exp(-x))


def _softplus(x):
    return jnp.maximum(x, 0.0) + jnp.log(1.0 + jnp.exp(-jnp.abs(x)))


def _rms(x, g):
    return x * lax.rsqrt(jnp.mean(x * x, axis=-1, keepdims=True) + RMS_EPS) * g


def _mm(a, b, prec=HI):
    return jnp.dot(a, b, preferred_element_type=F32, precision=prec)


def _mm_nt(a, b, prec=HI):
    return lax.dot_general(a, b, (((1,), (1,)), ((), ())), preferred_element_type=F32, precision=prec)


def _mm_tn(a, b, prec=HI):
    return lax.dot_general(a, b, (((0,), (0,)), ((), ())), preferred_element_type=F32, precision=prec)


def _bdot(a, b):
    return jnp.dot(a.astype(BF16), b, preferred_element_type=F32)


def _inproj_body(x_ref, g_ref, w_ref, o_ref):
    h = _rms(x_ref[...], g_ref[...])
    o_ref[...] = _bdot(h, w_ref[...])


def _inproj(x, g, w):
    rows = x.shape[0]
    tm = min(rows, 256)
    return pl.pallas_call(
        _inproj_body,
        grid=(rows // tm,),
        in_specs=[
            pl.BlockSpec((tm, D_MODEL), lambda i: (i, 0)),
            pl.BlockSpec((1, D_MODEL), lambda i: (0, 0)),
            pl.BlockSpec((D_MODEL, P_COLS), lambda i: (0, 0)),
        ],
        out_specs=pl.BlockSpec((tm, P_COLS), lambda i: (i, 0)),
        out_shape=jax.ShapeDtypeStruct((rows, P_COLS), F32),
        compiler_params=pltpu.CompilerParams(
            dimension_semantics=("parallel",), vmem_limit_bytes=VMEM_LIMIT),
        name="inproj",
    )(x, g, w)


FF_CHUNK = D_FF // 2


def _post_body(x_ref, m_ref, wo_ref, g1_ref, g2_ref, wg_ref, wu_ref, wd_ref, g3_ref, o_ref):
    mixed = _bdot(m_ref[...], wo_ref[...])
    x1 = x_ref[...] + _rms(mixed, g1_ref[...])
    h = _rms(x1, g2_ref[...]).astype(BF16)
    acc = jnp.zeros(x1.shape, F32)
    for c in range(D_FF // FF_CHUNK):
        sl = slice(c * FF_CHUNK, (c + 1) * FF_CHUNK)
        gate = jnp.dot(h, wg_ref[:, sl], preferred_element_type=F32)
        up = jnp.dot(h, wu_ref[:, sl], preferred_element_type=F32)
        act = gate * _sigmoid(gate) * up
        acc = acc + _bdot(act, wd_ref[sl, :])
    o_ref[...] = x1 + _rms(acc, g3_ref[...])


def _post(x, mixed, wo, g1, g2, wg, wu, wd, g3):
    rows = x.shape[0]
    tm = min(rows, 512)
    row_spec = pl.BlockSpec((tm, D_MODEL), lambda i: (i, 0))
    vec_spec = pl.BlockSpec((1, D_MODEL), lambda i: (0, 0))

    def resident(shape):
        return pl.BlockSpec(shape, lambda i: (0, 0), pipeline_mode=pl.Buffered(1))

    return pl.pallas_call(
        _post_body,
        grid=(rows // tm,),
        in_specs=[row_spec, row_spec, resident((D_MODEL, D_MODEL)), vec_spec, vec_spec,
                  resident((D_MODEL, D_FF)), resident((D_MODEL, D_FF)), resident((D_FF, D_MODEL)),
                  vec_spec],
        out_specs=row_spec,
        out_shape=jax.ShapeDtypeStruct((rows, D_MODEL), F32),
        compiler_params=pltpu.CompilerParams(
            dimension_semantics=("parallel",), vmem_limit_bytes=VMEM_LIMIT),
        name="post",
    )(x, mixed, wo, g1, g2, wg, wu, wd, g3)


def _inv_unit_lower(a, eye, chunk):
    n = -a
    p = eye + n
    for _ in range(max(int(math.ceil(math.log2(chunk))) - 1, 0)):
        n = _mm(n, n)
        p = p + _mm(p, n)
    return p


def _mixer_body(proj_ref, conv0_ref, shift0_ref, dn0_ref, rw0_ref, convw_ref, gdnp_ref, dnnorm_ref,
                mu_ref, rwvec_ref, wup_ref, aup_ref, gup_ref,
                mixed_ref, dn_out_ref, rw_out_ref,
                ext_qkv, ext_rw, dn_s, rw_s, *, chunk, n_pad):
    C = chunk
    C2 = 2 * C
    H = SUBLANES
    t = pl.program_id(1)

    @pl.when(t == 0)
    def _():
        ext_qkv[0:H, :] = conv0_ref[0]
        ext_rw[0:H, :] = shift0_ref[0]
        dn_s[...] = dn0_ref[0]
        rw_s[...] = rw0_ref[0]

    ri = lax.broadcasted_iota(jnp.int32, (C2, C2), 0)
    ci = lax.broadcasted_iota(jnp.int32, (C2, C2), 1)
    same = (ri >= C) == (ci >= C)
    strict = same & (ci < ri)
    incl = same & (ci <= ri)
    eye2 = (ri == ci).astype(F32)
    ones2 = jnp.ones((C2, C2), F32)
    r1 = lax.broadcasted_iota(jnp.int32, (C, C), 0)
    c1 = lax.broadcasted_iota(jnp.int32, (C, C), 1)
    lt_incl = (c1 <= r1).astype(F32)
    lane = lax.broadcasted_iota(jnp.int32, (1, LANES), 1)
    m0 = (lane < RW_HEAD_DIM).astype(F32)
    m1 = 1.0 - m0
    bi = lax.broadcasted_iota(jnp.int32, (LANES, LANES), 0)
    bj = lax.broadcasted_iota(jnp.int32, (LANES, LANES), 1)
    bd = ((bi >= RW_HEAD_DIM) == (bj >= RW_HEAD_DIM)).astype(F32)
    if n_pad:
        rowm = (lax.broadcasted_iota(jnp.int32, (C, 1), 0) >= n_pad).astype(F32)
    else:
        rowm = None

    def stack(a, b):
        return jnp.concatenate([a, b], axis=0)

    def row_bcast(col):
        return _mm(ones2, eye2 * col)

    x_qkv = proj_ref[:, C_QKV:C_QKV + DN_QKV]
    ext_qkv[H:H + C, :] = x_qkv
    cw = convw_ref[...]
    conv = (ext_qkv[H - 3:H - 3 + C, :] * cw[0:1, :] + ext_qkv[H - 2:H - 2 + C, :] * cw[1:2, :]
            + ext_qkv[H - 1:H - 1 + C, :] * cw[2:3, :] + x_qkv * cw[3:4, :])
    qkv = conv * _sigmoid(conv)
    ext_qkv[0:H, :] = ext_qkv[C:C + H, :]

    z = proj_ref[:, C_Z:C_Z + DN_WIDTH]
    bg = proj_ref[:, C_BG:C_BG + LANES]
    gp = gdnp_ref[...]
    beta_all = _sigmoid(bg)
    glog_all = -jnp.exp(gp[1:2, :]) * _softplus(bg + gp[0:1, :])
    if rowm is not None:
        beta_all = beta_all * rowm
        glog_all = glog_all * rowm
    gcum_all = _mm(lt_incl, glog_all)
    dn_w = dnnorm_ref[...]

    def l2n(x):
        return x * lax.rsqrt(jnp.sum(x * x, axis=-1, keepdims=True) + L2_EPS)

    for p in range(DN_HEADS // 2):
        hs = (2 * p, 2 * p + 1)
        q2 = stack(*[l2n(qkv[:, h * LANES:(h + 1) * LANES]) * (DN_HEAD_DIM ** -0.5) for h in hs])
        k2 = stack(*[l2n(qkv[:, DN_WIDTH + h * LANES:DN_WIDTH + (h + 1) * LANES]) for h in hs])
        v2 = stack(*[qkv[:, 2 * DN_WIDTH + h * LANES:2 * DN_WIDTH + (h + 1) * LANES] for h in hs])
        beta2 = stack(*[beta_all[:, h:h + 1] for h in hs])
        gc2 = stack(*[gcum_all[:, DN_HEADS + h:DN_HEADS + h + 1] for h in hs])
        diff = jnp.where(incl, gc2 - row_bcast(gc2), 0.0)
        dec = jnp.exp(diff)
        kb = k2 * beta2
        a_mat = jnp.where(strict, _mm_nt(kb, k2) * dec, 0.0)
        qk = jnp.where(incl, _mm_nt(q2, k2) * dec, 0.0)
        tm = _inv_unit_lower(a_mat, eye2, C)
        eg = jnp.exp(gc2)
        u = _mm(tm, v2 * beta2)
        w = _mm(tm, kb * eg)
        qg = q2 * eg
        v_new = []
        for e, h in enumerate(hs):
            sl = slice(e * C, (e + 1) * C)
            v_new.append(u[sl] - _mm(w[sl], dn_s[h]))
        o_intra = _mm(qk, stack(*v_new))
        for e, h in enumerate(hs):
            sl = slice(e * C, (e + 1) * C)
            s_old = dn_s[h]
            o = _mm(qg[sl], s_old) + o_intra[sl]
            g_last = gc2[(e + 1) * C - 1:(e + 1) * C, :]
            k_dec = k2[sl] * jnp.exp(g_last - gc2[sl])
            dn_s[h] = s_old * jnp.exp(g_last) + _mm_tn(k_dec, v_new[e])
            zh = z[:, h * LANES:(h + 1) * LANES]
            mixed_ref[:, h * LANES:(h + 1) * LANES] = _rms(o, dn_w) * (zh * _sigmoid(zh))

    cols = proj_ref[:, C_RW:C_RW + RW_COLS]
    ext_rw[H:H + C, :] = cols
    prev = ext_rw[H - 1:H - 1 + C, :]
    ext_rw[0:H, :] = ext_rw[C:C + H, :]
    xs = cols + (prev - cols) * mu_ref[...]
    r = xs[:, 0:RW_WIDTH]
    kr = xs[:, RW_WIDTH:2 * RW_WIDTH]
    vr = xs[:, 2 * RW_WIDTH:3 * RW_WIDTH]
    wa = xs[:, 3 * RW_WIDTH:3 * RW_WIDTH + LANES]
    gd = xs[:, 3 * RW_WIDTH + LANES:RW_COLS]
    rv = rwvec_ref[...]
    w0, a0, k_k, k_a, r_k, ln_w, ln_b = (rv[i:i + 1, :] for i in range(7))
    w_log = -_softplus(-(w0 + _mm(jnp.tanh(wa), wup_ref[...]))) - 0.5
    lw = -jnp.exp(w_log)
    a = _sigmoid(a0 + _mm(wa, aup_ref[...]))
    g = _mm(_sigmoid(gd), gup_ref[...])
    kk_raw = kr * k_k
    kr2 = kr * (1.0 + (a - 1.0) * k_a)
    bonus_in = r * kr2 * r_k
    if rowm is not None:
        lw = lw * rowm
        kr2 = kr2 * rowm
    cl = _mm(lt_incl, lw)
    e_cl = jnp.exp(cl)
    e_prev = jnp.exp(cl - lw)
    e_inv = jnp.exp(-cl)
    r_t = r * e_cl
    k_t = kr2 * e_inv
    p_last = e_cl[C - 1:C, :]

    for p in range(RW_HEADS // 2):
        sl = slice(p * LANES, (p + 1) * LANES)
        s_old = rw_s[p]
        kk = kk_raw[:, sl]
        kk = kk * lax.rsqrt(_mm(kk * kk, bd) + L2_EPS)
        if rowm is not None:
            kk = kk * rowm
        kk_t = kk * e_prev[:, sl]
        a_t = kk * a[:, sl] * e_inv[:, sl]
        rt_p = r_t[:, sl]
        kt_p = k_t[:, sl]
        v_p = vr[:, sl]
        l_kk = stack(kk_t * m0, kk_t * m1)
        l_r = stack(rt_p * m0, rt_p * m1)
        r_a = stack(a_t, a_t)
        r_k2 = stack(kt_p, kt_p)
        m_ka = jnp.where(strict, _mm_nt(l_kk, r_a), 0.0)
        m_kv = jnp.where(strict, _mm_nt(l_kk, r_k2), 0.0)
        a_ra = jnp.where(incl, _mm_nt(l_r, r_a), 0.0)
        a_rk = jnp.where(incl, _mm_nt(l_r, r_k2), 0.0)
        tm = _inv_unit_lower(m_ka, eye2, C)
        x0 = _mm_nt(kk_t, s_old)
        y0 = _mm_nt(rt_p, s_old)
        vs = stack(v_p * m0, v_p * m1)
        us = _mm(tm, stack(x0 * m0, x0 * m1) + _mm(m_kv, vs))
        ys = stack(y0 * m0, y0 * m1) + _mm(a_rk, vs) - _mm(a_ra, us)
        y = ys[0:C] + ys[C:C2]
        u1 = us[0:C] + us[C:C2]
        rw_s[p] = (s_old + _mm_tn(v_p, kt_p) - _mm_tn(u1, a_t)) * bd * p_last[:, sl]
        mu_g = _mm(y, bd) * (1.0 / RW_HEAD_DIM)
        yc = y - mu_g
        var = _mm(yc * yc, bd) * (1.0 / RW_HEAD_DIM)
        yn = yc * lax.rsqrt(var + GN_EPS) * ln_w[:, sl] + ln_b[:, sl]
        bonus = _mm(bonus_in[:, sl], bd) * v_p
        mixed_ref[:, DN_WIDTH + p * LANES:DN_WIDTH + (p + 1) * LANES] = (yn + bonus) * g[:, sl]

    @pl.when(t == pl.num_programs(1) - 1)
    def _():
        dn_out_ref[0] = dn_s[...]
        rw_out_ref[0] = rw_s[...]


def _mixer(proj, conv0, shift0, dn0, rw0, lp, *, batch, seq, chunk, n_pad):
    nt = seq // chunk
    bsel = (lambda b: b) if conv0.shape[0] == batch else (lambda b: 0)
    full = lambda shape: pl.BlockSpec(shape, lambda b, t: (0,) * len(shape))
    st_spec = lambda: pl.BlockSpec((1, 4, LANES, LANES), lambda b, t: (bsel(b), 0, 0, 0))
    out_st = lambda: pl.BlockSpec((1, 4, LANES, LANES), lambda b, t: (b, 0, 0, 0))
    st_shape = jax.ShapeDtypeStruct((batch, 4, LANES, LANES), F32)
    return pl.pallas_call(
        functools.partial(_mixer_body, chunk=chunk, n_pad=n_pad),
        grid=(batch, nt),
        in_specs=[
            pl.BlockSpec((chunk, P_COLS), lambda b, t: (b * nt + t, 0)),
            pl.BlockSpec((1, SUBLANES, DN_QKV), lambda b, t: (bsel(b), 0, 0)),
            pl.BlockSpec((1, SUBLANES, RW_COLS), lambda b, t: (bsel(b), 0, 0)),
            st_spec(), st_spec(),
            full((SUBLANES, DN_QKV)), full((SUBLANES, LANES)), full((1, LANES)),
            full((1, RW_COLS)), full((SUBLANES, RW_WIDTH)),
            full((LANES, RW_WIDTH)), full((LANES, RW_WIDTH)), full((G_LORA, RW_WIDTH)),
        ],
        out_specs=[
            pl.BlockSpec((chunk, D_MODEL), lambda b, t: (b * nt + t, 0)),
            out_st(), out_st(),
        ],
        out_shape=[jax.ShapeDtypeStruct((batch * seq, D_MODEL), F32), st_shape, st_shape],
        scratch_shapes=[
            pltpu.VMEM((SUBLANES + chunk, DN_QKV), F32),
            pltpu.VMEM((SUBLANES + chunk, RW_COLS), F32),
            pltpu.VMEM((4, LANES, LANES), F32),
            pltpu.VMEM((4, LANES, LANES), F32),
        ],
        compiler_params=pltpu.CompilerParams(
            dimension_semantics=("parallel", "arbitrary"), vmem_limit_bytes=VMEM_LIMIT),
        name="mixer",
    )(proj, conv0, shift0, dn0, rw0, lp["conv_w"], lp["gdnp"], lp["dn_norm"], lp["mu"],
      lp["rwvec"], lp["w_up"], lp["a_up"], lp["g_up"])


def _rw_to_pairs(s):
    b = s.shape[0]
    sp = s.reshape(b, RW_HEADS // 2, 2, RW_HEAD_DIM, RW_HEAD_DIM)
    zero = jnp.zeros_like(sp[:, :, 0])
    top = jnp.concatenate([sp[:, :, 0], zero], axis=-1)
    bot = jnp.concatenate([zero, sp[:, :, 1]], axis=-1)
    return jnp.concatenate([top, bot], axis=-2)


def _rw_from_pairs(sp):
    b = sp.shape[0]
    d = RW_HEAD_DIM
    return jnp.stack([sp[:, :, :d, :d], sp[:, :, d:, d:]], axis=2).reshape(b, RW_HEADS, d, d)


def _pad_rows_front(a, rows):
    return jnp.pad(a, ((0, 0), (rows - a.shape[1], 0), (0, 0)))


def _prep_layer_params(l, w_in, dn_conv_w, dn_a_log, dn_dt_bias, dn_norm_w, rw_mu, rw_w0, rw_w_up,
                       rw_a0, rw_a_up, rw_g_up, rw_k_k, rw_k_a, rw_r_k, rw_ln_w, rw_ln_b):
    off = DN_QKV + DN_WIDTH
    wi = w_in[l]
    w_in_p = jnp.concatenate(
        [wi[:, :off + 2 * DN_HEADS], jnp.zeros((D_MODEL, LANES - 2 * DN_HEADS), F32),
         wi[:, off + 2 * DN_HEADS:]], axis=1).astype(BF16)
    gdnp = jnp.zeros((SUBLANES, LANES), F32)
    gdnp = gdnp.at[0, DN_HEADS:2 * DN_HEADS].set(dn_dt_bias[l])
    gdnp = gdnp.at[1, DN_HEADS:2 * DN_HEADS].set(dn_a_log[l])
    rwvec = jnp.stack([rw_w0[l], rw_a0[l], rw_k_k[l], rw_k_a[l], rw_r_k[l].reshape(RW_WIDTH),
                       rw_ln_w[l], rw_ln_b[l], jnp.zeros((RW_WIDTH,), F32)])
    zeros_lora = jnp.zeros((W_LORA, RW_WIDTH), F32)
    return {
        "w_in": w_in_p,
        "conv_w": jnp.concatenate([dn_conv_w[l], jnp.zeros((SUBLANES - CONV_W, DN_QKV), F32)], axis=0),
        "gdnp": gdnp,
        "dn_norm": dn_norm_w[l].reshape(1, LANES),
        "mu": rw_mu[l].reshape(1, RW_COLS),
        "rwvec": rwvec,
        "w_up": jnp.concatenate([rw_w_up[l], zeros_lora], axis=0),
        "a_up": jnp.concatenate([zeros_lora, rw_a_up[l]], axis=0),
        "g_up": rw_g_up[l],
    }


def _trunk(x, states, layers, ffn, *, batch, seq, chunk, n_pad):
    seq_real = seq - n_pad
    new_states = []
    for l in range(DEPTH):
        lp = layers[l]
        conv_s, dn_s, shift_s, rw_s = states[l]
        proj = _inproj(x, ffn["g_pre_mix"][l], lp["w_in"])
        proj3 = proj.reshape(batch, seq_real, P_COLS)
        if n_pad:
            b0 = conv_s.shape[0]
            hist = jnp.zeros((b0, n_pad, P_COLS), F32)
            hist = hist.at[:, n_pad - (CONV_W - 1):, C_QKV:C_QKV + DN_QKV].set(conv_s)
            hist = hist.at[:, n_pad - 1, C_RW:].set(shift_s)
            mix_in = jnp.concatenate([hist, proj3], axis=1).reshape(batch * seq, P_COLS)
        else:
            mix_in = proj
        conv0 = _pad_rows_front(conv_s, SUBLANES)
        shift0 = _pad_rows_front(shift_s[:, None, :], SUBLANES)
        mixed, dn_new, rw_new = _mixer(mix_in, conv0, shift0, dn_s, rw_s, lp,
                                       batch=batch, seq=seq, chunk=chunk, n_pad=n_pad)
        if n_pad:
            mixed = mixed.reshape(batch, seq, D_MODEL)[:, n_pad:].reshape(batch * seq_real, D_MODEL)
        x = _post(x, mixed, ffn["w_out"][l], ffn["g_post_mix"][l], ffn["g_pre_ffn"][l],
                  ffn["w_gate"][l], ffn["w_up"][l], ffn["w_down"][l], ffn["g_post_ffn"][l])
        conv_new = proj3[:, seq_real - (CONV_W - 1):, C_QKV:C_QKV + DN_QKV]
        shift_new = proj3[:, seq_real - 1, C_RW:]
        new_states.append((conv_new, dn_new, shift_new, rw_new))
    return x, new_states


def kernel(x_prompt, x_sample, state_dn_conv, state_dn, state_rw_shift, state_rw, meta, g_pre_mix, g_post_mix, g_pre_ffn, g_post_ffn, w_in, dn_conv_w, dn_a_log, dn_dt_bias, dn_norm_w, rw_mu, rw_w0, rw_w_up, rw_a0, rw_a_up, rw_g_up, rw_k_k, rw_k_a, rw_r_k, rw_ln_w, rw_ln_b, w_out, w_gate, w_up, w_down):
    batch, seq, _ = x_prompt.shape
    dec_batch, dec_seq, _ = x_sample.shape
    layers = [_prep_layer_params(l, w_in, dn_conv_w, dn_a_log, dn_dt_bias, dn_norm_w, rw_mu, rw_w0,
                                 rw_w_up, rw_a0, rw_a_up, rw_g_up, rw_k_k, rw_k_a, rw_r_k, rw_ln_w,
                                 rw_ln_b) for l in range(DEPTH)]
    vec = lambda g: g.reshape(DEPTH, 1, D_MODEL)
    ffn = {"g_pre_mix": vec(g_pre_mix), "g_post_mix": vec(g_post_mix), "g_pre_ffn": vec(g_pre_ffn),
           "g_post_ffn": vec(g_post_ffn), "w_out": w_out.astype(BF16), "w_gate": w_gate.astype(BF16),
           "w_up": w_up.astype(BF16), "w_down": w_down.astype(BF16)}

    zero_states = [(jnp.zeros((1, CONV_W - 1, DN_QKV), F32), jnp.zeros((1, 4, LANES, LANES), F32),
                    jnp.zeros((1, RW_COLS), F32), jnp.zeros((1, 4, LANES, LANES), F32))] * DEPTH
    _, meta_states = _trunk(meta.astype(F32), zero_states, layers, ffn,
                            batch=1, seq=N_META, chunk=N_META, n_pad=0)

    yp, ps = _trunk(x_prompt.reshape(batch * seq, D_MODEL), meta_states, layers, ffn,
                    batch=batch, seq=seq, chunk=PROMPT_CHUNK, n_pad=0)

    n_pad = SUBLANES - dec_seq
    sample_states = [(state_dn_conv[l], state_dn[l], state_rw_shift[l], _rw_to_pairs(state_rw[l]))
                     for l in range(DEPTH)]
    ys, ss = _trunk(x_sample.reshape(dec_batch * dec_seq, D_MODEL), sample_states, layers, ffn,
                    batch=dec_batch, seq=SUBLANES, chunk=SUBLANES, n_pad=n_pad)

    def collect(sts):
        return (jnp.stack([s[0] for s in sts]), jnp.stack([s[1] for s in sts]),
                jnp.stack([s[2] for s in sts]), jnp.stack([_rw_from_pairs(s[3]) for s in sts]))

    cp, dp, sp, rp = collect(ps)
    cs, ds, sh, rs = collect(ss)
    return (yp.reshape(batch, seq, D_MODEL), ys.reshape(dec_batch, dec_seq, D_MODEL),
            cp, dp, sp, rp, cs, ds, sh, rs)
```

```python
import functools
import math

import jax
import jax.numpy as jnp
from jax import lax
from jax.experimental import pallas as pl
from jax.experimental.pallas import tpu as pltpu

F32 = jnp.float32
BF16 = jnp.bfloat16

D_MODEL = 1024
DEPTH = 4
N_META = 16
DN_HEADS = 4
DN_HEAD_DIM = 128
DN_WIDTH = DN_HEADS * DN_HEAD_DIM
DN_QKV = 3 * DN_WIDTH
RW_HEADS = 8
RW_HEAD_DIM = 64
RW_WIDTH = RW_HEADS * RW_HEAD_DIM
W_LORA = 64
A_LORA = 64
G_LORA = 128
RW_COLS = 3 * RW_WIDTH + W_LORA + A_LORA + G_LORA
D_FF = 2816
CONV_W = 4
RMS_EPS = 1e-6
L2_EPS = 1e-6
GN_EPS = 64e-5

LANES = 128
SUBLANES = 8
VMEM_LIMIT = 56 * 1024 * 1024

C_QKV = 0
C_Z = DN_QKV
C_BG = C_Z + DN_WIDTH
C_RW = C_BG + LANES
P_COLS = C_RW + RW_COLS
PROMPT_CHUNK = 64


def _sigmoid(x):
    return 1.0 / (1.0 + jnp.exp(-x))


def _softplus(x):
    return jnp.maximum(x, 0.0) + jnp.log(1.0 + jnp.exp(-jnp.abs(x)))


def _rms(x, g):
    return x * lax.rsqrt(jnp.mean(x * x, axis=-1, keepdims=True) + RMS_EPS) * g


def _mm(a, b):
    return jnp.dot(a.astype(BF16), b.astype(BF16), preferred_element_type=F32)


def _mm_nt(a, b):
    return lax.dot_general(a.astype(BF16), b.astype(BF16), (((1,), (1,)), ((), ())),
                           preferred_element_type=F32)


def _mm_tn(a, b):
    return lax.dot_general(a.astype(BF16), b.astype(BF16), (((0,), (0,)), ((), ())),
                           preferred_element_type=F32)


def _mm_sel(sel, x):
    sel = sel.astype(BF16)
    x1 = x.astype(BF16)
    r1 = x - x1.astype(F32)
    x2 = r1.astype(BF16)
    x3 = (r1 - x2.astype(F32)).astype(BF16)
    dot = lambda t: jnp.dot(sel, t, preferred_element_type=F32)
    return dot(x1) + dot(x2) + dot(x3)


def _bdot(a, b):
    return jnp.dot(a.astype(BF16), b, preferred_element_type=F32)


def _inproj_body(x_ref, g_ref, w_ref, o_ref):
    h = _rms(x_ref[...], g_ref[...])
    o_ref[...] = _bdot(h, w_ref[...])


def _inproj(x, g, w):
    rows = x.shape[0]
    tm = min(rows, 256)
    return pl.pallas_call(
        _inproj_body,
        grid=(rows // tm,),
        in_specs=[
            pl.BlockSpec((tm, D_MODEL), lambda i: (i, 0)),
            pl.BlockSpec((1, D_MODEL), lambda i: (0, 0)),
            pl.BlockSpec((D_MODEL, P_COLS), lambda i: (0, 0)),
        ],
        out_specs=pl.BlockSpec((tm, P_COLS), lambda i: (i, 0)),
        out_shape=jax.ShapeDtypeStruct((rows, P_COLS), F32),
        compiler_params=pltpu.CompilerParams(
            dimension_semantics=("parallel",), vmem_limit_bytes=VMEM_LIMIT),
        name="inproj",
    )(x, g, w)


FF_CHUNK = D_FF // 2


def _post_body(x_ref, m_ref, wo_ref, g1_ref, g2_ref, wg_ref, wu_ref, wd_ref, g3_ref, o_ref):
    mixed = _bdot(m_ref[...], wo_ref[...])
    x1 = x_ref[...] + _rms(mixed, g1_ref[...])
    h = _rms(x1, g2_ref[...]).astype(BF16)
    acc = jnp.zeros(x1.shape, F32)
    for c in range(D_FF // FF_CHUNK):
        sl = slice(c * FF_CHUNK, (c + 1) * FF_CHUNK)
        gate = jnp.dot(h, wg_ref[:, sl], preferred_element_type=F32)
        up = jnp.dot(h, wu_ref[:, sl], preferred_element_type=F32)
        act = gate * _sigmoid(gate) * up
        acc = acc + _bdot(act, wd_ref[sl, :])
    o_ref[...] = x1 + _rms(acc, g3_ref[...])


def _post(x, mixed, wo, g1, g2, wg, wu, wd, g3):
    rows = x.shape[0]
    tm = min(rows, 512)
    row_spec = pl.BlockSpec((tm, D_MODEL), lambda i: (i, 0))
    vec_spec = pl.BlockSpec((1, D_MODEL), lambda i: (0, 0))

    def resident(shape):
        return pl.BlockSpec(shape, lambda i: (0, 0), pipeline_mode=pl.Buffered(1))

    return pl.pallas_call(
        _post_body,
        grid=(rows // tm,),
        in_specs=[row_spec, row_spec, resident((D_MODEL, D_MODEL)), vec_spec, vec_spec,
                  resident((D_MODEL, D_FF)), resident((D_MODEL, D_FF)), resident((D_FF, D_MODEL)),
                  vec_spec],
        out_specs=row_spec,
        out_shape=jax.ShapeDtypeStruct((rows, D_MODEL), F32),
        compiler_params=pltpu.CompilerParams(
            dimension_semantics=("parallel",), vmem_limit_bytes=VMEM_LIMIT),
        name="post",
    )(x, mixed, wo, g1, g2, wg, wu, wd, g3)


def _inv_unit_lower(a, eye, chunk):
    n = -a
    p = eye + n
    for _ in range(max(int(math.ceil(math.log2(chunk))) - 1, 0)):
        n = _mm(n, n)
        p = p + _mm(p, n)
    return p


def _mixer_body(proj_ref, conv0_ref, shift0_ref, dn0_ref, rw0_ref, convw_ref, gdnp_ref, dnnorm_ref,
                mu_ref, rwvec_ref, wup_ref, aup_ref, gup_ref,
                mixed_ref, dn_out_ref, rw_out_ref,
                ext_qkv, ext_rw, dn_s, rw_s, *, chunk, n_pad):
    C = chunk
    C2 = 2 * C
    H = SUBLANES
    t = pl.program_id(1)

    @pl.when(t == 0)
    def _():
        ext_qkv[0:H, :] = conv0_ref[0]
        ext_rw[0:H, :] = shift0_ref[0]
        dn_s[...] = dn0_ref[0]
        rw_s[...] = rw0_ref[0]

    ri = lax.broadcasted_iota(jnp.int32, (C2, C2), 0)
    ci = lax.broadcasted_iota(jnp.int32, (C2, C2), 1)
    same = (ri >= C) == (ci >= C)
    strict = same & (ci < ri)
    incl = same & (ci <= ri)
    eye2 = (ri == ci).astype(F32)
    ones2 = jnp.ones((C2, C2), F32)
    r1 = lax.broadcasted_iota(jnp.int32, (C, C), 0)
    c1 = lax.broadcasted_iota(jnp.int32, (C, C), 1)
    lt_incl = (c1 <= r1).astype(F32)
    lane = lax.broadcasted_iota(jnp.int32, (1, LANES), 1)
    m0 = (lane < RW_HEAD_DIM).astype(F32)
    m1 = 1.0 - m0
    bi = lax.broadcasted_iota(jnp.int32, (LANES, LANES), 0)
    bj = lax.broadcasted_iota(jnp.int32, (LANES, LANES), 1)
    bd = ((bi >= RW_HEAD_DIM) == (bj >= RW_HEAD_DIM)).astype(F32)
    if n_pad:
        rowm = (lax.broadcasted_iota(jnp.int32, (C, 1), 0) >= n_pad).astype(F32)
    else:
        rowm = None

    def stack(a, b):
        return jnp.concatenate([a, b], axis=0)

    def row_bcast(col):
        return _mm_sel(ones2, eye2 * col)

    x_qkv = proj_ref[:, C_QKV:C_QKV + DN_QKV]
    ext_qkv[H:H + C, :] = x_qkv
    cw = convw_ref[...]
    conv = (ext_qkv[H - 3:H - 3 + C, :] * cw[0:1, :] + ext_qkv[H - 2:H - 2 + C, :] * cw[1:2, :]
            + ext_qkv[H - 1:H - 1 + C, :] * cw[2:3, :] + x_qkv * cw[3:4, :])
    qkv = conv * _sigmoid(conv)
    ext_qkv[0:H, :] = ext_qkv[C:C + H, :]

    z = proj_ref[:, C_Z:C_Z + DN_WIDTH]
    bg = proj_ref[:, C_BG:C_BG + LANES]
    gp = gdnp_ref[...]
    beta_all = _sigmoid(bg)
    glog_all = -jnp.exp(gp[1:2, :]) * _softplus(bg + gp[0:1, :])
    if rowm is not None:
        beta_all = beta_all * rowm
        glog_all = glog_all * rowm
    gcum_all = _mm_sel(lt_incl, glog_all)
    dn_w = dnnorm_ref[...]

    def l2n(x):
        return x * lax.rsqrt(jnp.sum(x * x, axis=-1, keepdims=True) + L2_EPS)

    for p in range(DN_HEADS // 2):
        hs = (2 * p, 2 * p + 1)
        q2 = stack(*[l2n(qkv[:, h * LANES:(h + 1) * LANES]) * (DN_HEAD_DIM ** -0.5) for h in hs])
        k2 = stack(*[l2n(qkv[:, DN_WIDTH + h * LANES:DN_WIDTH + (h + 1) * LANES]) for h in hs])
        v2 = stack(*[qkv[:, 2 * DN_WIDTH + h * LANES:2 * DN_WIDTH + (h + 1) * LANES] for h in hs])
        beta2 = stack(*[beta_all[:, h:h + 1] for h in hs])
        gc2 = stack(*[gcum_all[:, DN_HEADS + h:DN_HEADS + h + 1] for h in hs])
        diff = jnp.where(incl, gc2 - row_bcast(gc2), 0.0)
        dec = jnp.exp(diff)
        kb = k2 * beta2
        a_mat = jnp.where(strict, _mm_nt(kb, k2) * dec, 0.0)
        qk = jnp.where(incl, _mm_nt(q2, k2) * dec, 0.0)
        tm = _inv_unit_lower(a_mat, eye2, C)
        eg = jnp.exp(gc2)
        u = _mm(tm, v2 * beta2)
        w = _mm(tm, kb * eg)
        qg = q2 * eg
        v_new = []
        for e, h in enumerate(hs):
            sl = slice(e * C, (e + 1) * C)
            v_new.append(u[sl] - _mm(w[sl], dn_s[h]))
        o_intra = _mm(qk, stack(*v_new))
        for e, h in enumerate(hs):
            sl = slice(e * C, (e + 1) * C)
            s_old = dn_s[h]
            o = _mm(qg[sl], s_old) + o_intra[sl]
            g_last = gc2[(e + 1) * C - 1:(e + 1) * C, :]
            k_dec = k2[sl] * jnp.exp(g_last - gc2[sl])
            dn_s[h] = s_old * jnp.exp(g_last) + _mm_tn(k_dec, v_new[e])
            zh = z[:, h * LANES:(h + 1) * LANES]
            mixed_ref[:, h * LANES:(h + 1) * LANES] = _rms(o, dn_w) * (zh * _sigmoid(zh))

    cols = proj_ref[:, C_RW:C_RW + RW_COLS]
    ext_rw[H:H + C, :] = cols
    prev = ext_rw[H - 1:H - 1 + C, :]
    ext_rw[0:H, :] = ext_rw[C:C + H, :]
    xs = cols + (prev - cols) * mu_ref[...]
    r = xs[:, 0:RW_WIDTH]
    kr = xs[:, RW_WIDTH:2 * RW_WIDTH]
    vr = xs[:, 2 * RW_WIDTH:3 * RW_WIDTH]
    wa = xs[:, 3 * RW_WIDTH:3 * RW_WIDTH + LANES]
    gd = xs[:, 3 * RW_WIDTH + LANES:RW_COLS]
    rv = rwvec_ref[...]
    w0, a0, k_k, k_a, r_k, ln_w, ln_b = (rv[i:i + 1, :] for i in range(7))
    w_log = -_softplus(-(w0 + _mm(jnp.tanh(wa), wup_ref[...]))) - 0.5
    lw = -jnp.exp(w_log)
    a = _sigmoid(a0 + _mm(wa, aup_ref[...]))
    g = _mm(_sigmoid(gd), gup_ref[...])
    kk_raw = kr * k_k
    kr2 = kr * (1.0 + (a - 1.0) * k_a)
    bonus_in = r * kr2 * r_k
    if rowm is not None:
        lw = lw * rowm
        kr2 = kr2 * rowm
    cl = _mm_sel(lt_incl, lw)
    e_cl = jnp.exp(cl)
    e_prev = jnp.exp(cl - lw)
    e_inv = jnp.exp(-cl)
    r_t = r * e_cl
    k_t = kr2 * e_inv
    p_last = e_cl[C - 1:C, :]

    for p in range(RW_HEADS // 2):
        sl = slice(p * LANES, (p + 1) * LANES)
        s_old = rw_s[p]
        kk = kk_raw[:, sl]
        kk = kk * lax.rsqrt(_mm(kk * kk, bd) + L2_EPS)
        if rowm is not None:
            kk = kk * rowm
        kk_t = kk * e_prev[:, sl]
        a_t = kk * a[:, sl] * e_inv[:, sl]
        rt_p = r_t[:, sl]
        kt_p = k_t[:, sl]
        v_p = vr[:, sl]
        l_kk = stack(kk_t * m0, kk_t * m1)
        l_r = stack(rt_p * m0, rt_p * m1)
        r_a = stack(a_t, a_t)
        r_k2 = stack(kt_p, kt_p)
        m_ka = jnp.where(strict, _mm_nt(l_kk, r_a), 0.0)
        m_kv = jnp.where(strict, _mm_nt(l_kk, r_k2), 0.0)
        a_ra = jnp.where(incl, _mm_nt(l_r, r_a), 0.0)
        a_rk = jnp.where(incl, _mm_nt(l_r, r_k2), 0.0)
        tm = _inv_unit_lower(m_ka, eye2, C)
        x0 = _mm_nt(kk_t, s_old)
        y0 = _mm_nt(rt_p, s_old)
        vs = stack(v_p * m0, v_p * m1)
        us = _mm(tm, stack(x0 * m0, x0 * m1) + _mm(m_kv, vs))
        ys = stack(y0 * m0, y0 * m1) + _mm(a_rk, vs) - _mm(a_ra, us)
        y = ys[0:C] + ys[C:C2]
        u1 = us[0:C] + us[C:C2]
        rw_s[p] = (s_old + _mm_tn(v_p, kt_p) - _mm_tn(u1, a_t)) * bd * p_last[:, sl]
        mu_g = _mm(y, bd) * (1.0 / RW_HEAD_DIM)
        yc = y - mu_g
        var = _mm(yc * yc, bd) * (1.0 / RW_HEAD_DIM)
        yn = yc * lax.rsqrt(var + GN_EPS) * ln_w[:, sl] + ln_b[:, sl]
        bonus = _mm(bonus_in[:, sl], bd) * v_p
        mixed_ref[:, DN_WIDTH + p * LANES:DN_WIDTH + (p + 1) * LANES] = (yn + bonus) * g[:, sl]

    @pl.when(t == pl.num_programs(1) - 1)
    def _():
        dn_out_ref[0] = dn_s[...]
        rw_out_ref[0] = rw_s[...]


def _mixer(proj, conv0, shift0, dn0, rw0, lp, *, batch, seq, chunk, n_pad):
    nt = seq // chunk
    bsel = (lambda b: b) if conv0.shape[0] == batch else (lambda b: 0)
    full = lambda shape: pl.BlockSpec(shape, lambda b, t: (0,) * len(shape))
    st_spec = lambda: pl.BlockSpec((1, 4, LANES, LANES), lambda b, t: (bsel(b), 0, 0, 0))
    out_st = lambda: pl.BlockSpec((1, 4, LANES, LANES), lambda b, t: (b, 0, 0, 0))
    st_shape = jax.ShapeDtypeStruct((batch, 4, LANES, LANES), F32)
    return pl.pallas_call(
        functools.partial(_mixer_body, chunk=chunk, n_pad=n_pad),
        grid=(batch, nt),
        in_specs=[
            pl.BlockSpec((chunk, P_COLS), lambda b, t: (b * nt + t, 0)),
            pl.BlockSpec((1, SUBLANES, DN_QKV), lambda b, t: (bsel(b), 0, 0)),
            pl.BlockSpec((1, SUBLANES, RW_COLS), lambda b, t: (bsel(b), 0, 0)),
            st_spec(), st_spec(),
            full((SUBLANES, DN_QKV)), full((SUBLANES, LANES)), full((1, LANES)),
            full((1, RW_COLS)), full((SUBLANES, RW_WIDTH)),
            full((LANES, RW_WIDTH)), full((LANES, RW_WIDTH)), full((G_LORA, RW_WIDTH)),
        ],
        out_specs=[
            pl.BlockSpec((chunk, D_MODEL), lambda b, t: (b * nt + t, 0)),
            out_st(), out_st(),
        ],
        out_shape=[jax.ShapeDtypeStruct((batch * seq, D_MODEL), F32), st_shape, st_shape],
        scratch_shapes=[
            pltpu.VMEM((SUBLANES + chunk, DN_QKV), F32),
            pltpu.VMEM((SUBLANES + chunk, RW_COLS), F32),
            pltpu.VMEM((4, LANES, LANES), F32),
            pltpu.VMEM((4, LANES, LANES), F32),
        ],
        compiler_params=pltpu.CompilerParams(
            dimension_semantics=("parallel", "arbitrary"), vmem_limit_bytes=VMEM_LIMIT),
        name="mixer",
    )(proj, conv0, shift0, dn0, rw0, lp["conv_w"], lp["gdnp"], lp["dn_norm"], lp["mu"],
      lp["rwvec"], lp["w_up"], lp["a_up"], lp["g_up"])


def _rw_to_pairs(s):
    b = s.shape[0]
    sp = s.reshape(b, RW_HEADS // 2, 2, RW_HEAD_DIM, RW_HEAD_DIM)
    zero = jnp.zeros_like(sp[:, :, 0])
    top = jnp.concatenate([sp[:, :, 0], zero], axis=-1)
    bot = jnp.concatenate([zero, sp[:, :, 1]], axis=-1)
    return jnp.concatenate([top, bot], axis=-2)


def _rw_from_pairs(sp):
    b = sp.shape[0]
    d = RW_HEAD_DIM
    return jnp.stack([sp[:, :, :d, :d], sp[:, :, d:, d:]], axis=2).reshape(b, RW_HEADS, d, d)


def _pad_rows_front(a, rows):
    return jnp.pad(a, ((0, 0), (rows - a.shape[1], 0), (0, 0)))


def _prep_layer_params(l, w_in, dn_conv_w, dn_a_log, dn_dt_bias, dn_norm_w, rw_mu, rw_w0, rw_w_up,
                       rw_a0, rw_a_up, rw_g_up, rw_k_k, rw_k_a, rw_r_k, rw_ln_w, rw_ln_b):
    off = DN_QKV + DN_WIDTH
    wi = w_in[l]
    w_in_p = jnp.concatenate(
        [wi[:, :off + 2 * DN_HEADS], jnp.zeros((D_MODEL, LANES - 2 * DN_HEADS), F32),
         wi[:, off + 2 * DN_HEADS:]], axis=1).astype(BF16)
    gdnp = jnp.zeros((SUBLANES, LANES), F32)
    gdnp = gdnp.at[0, DN_HEADS:2 * DN_HEADS].set(dn_dt_bias[l])
    gdnp = gdnp.at[1, DN_HEADS:2 * DN_HEADS].set(dn_a_log[l])
    rwvec = jnp.stack([rw_w0[l], rw_a0[l], rw_k_k[l], rw_k_a[l], rw_r_k[l].reshape(RW_WIDTH),
                       rw_ln_w[l], rw_ln_b[l], jnp.zeros((RW_WIDTH,), F32)])
    zeros_lora = jnp.zeros((W_LORA, RW_WIDTH), F32)
    return {
        "w_in": w_in_p,
        "conv_w": jnp.concatenate([dn_conv_w[l], jnp.zeros((SUBLANES - CONV_W, DN_QKV), F32)], axis=0),
        "gdnp": gdnp,
        "dn_norm": dn_norm_w[l].reshape(1, LANES),
        "mu": rw_mu[l].reshape(1, RW_COLS),
        "rwvec": rwvec,
        "w_up": jnp.concatenate([rw_w_up[l], zeros_lora], axis=0),
        "a_up": jnp.concatenate([zeros_lora, rw_a_up[l]], axis=0),
        "g_up": rw_g_up[l],
    }


def _trunk(x, states, layers, ffn, *, batch, seq, chunk, n_pad):
    seq_real = seq - n_pad
    new_states = []
    for l in range(DEPTH):
        lp = layers[l]
        conv_s, dn_s, shift_s, rw_s = states[l]
        proj = _inproj(x, ffn["g_pre_mix"][l], lp["w_in"])
        proj3 = proj.reshape(batch, seq_real, P_COLS)
        if n_pad:
            b0 = conv_s.shape[0]
            hist = jnp.zeros((b0, n_pad, P_COLS), F32)
            hist = hist.at[:, n_pad - (CONV_W - 1):, C_QKV:C_QKV + DN_QKV].set(conv_s)
            hist = hist.at[:, n_pad - 1, C_RW:].set(shift_s)
            mix_in = jnp.concatenate([hist, proj3], axis=1).reshape(batch * seq, P_COLS)
        else:
            mix_in = proj
        conv0 = _pad_rows_front(conv_s, SUBLANES)
        shift0 = _pad_rows_front(shift_s[:, None, :], SUBLANES)
        mixed, dn_new, rw_new = _mixer(mix_in, conv0, shift0, dn_s, rw_s, lp,
                                       batch=batch, seq=seq, chunk=chunk, n_pad=n_pad)
        if n_pad:
            mixed = mixed.reshape(batch, seq, D_MODEL)[:, n_pad:].reshape(batch * seq_real, D_MODEL)
        x = _post(x, mixed, ffn["w_out"][l], ffn["g_post_mix"][l], ffn["g_pre_ffn"][l],
                  ffn["w_gate"][l], ffn["w_up"][l], ffn["w_down"][l], ffn["g_post_ffn"][l])
        conv_new = proj3[:, seq_real - (CONV_W - 1):, C_QKV:C_QKV + DN_QKV]
        shift_new = proj3[:, seq_real - 1, C_RW:]
        new_states.append((conv_new, dn_new, shift_new, rw_new))
    return x, new_states


def kernel(x_prompt, x_sample, state_dn_conv, state_dn, state_rw_shift, state_rw, meta, g_pre_mix, g_post_mix, g_pre_ffn, g_post_ffn, w_in, dn_conv_w, dn_a_log, dn_dt_bias, dn_norm_w, rw_mu, rw_w0, rw_w_up, rw_a0, rw_a_up, rw_g_up, rw_k_k, rw_k_a, rw_r_k, rw_ln_w, rw_ln_b, w_out, w_gate, w_up, w_down):
    batch, seq, _ = x_prompt.shape
    dec_batch, dec_seq, _ = x_sample.shape
    layers = [_prep_layer_params(l, w_in, dn_conv_w, dn_a_log, dn_dt_bias, dn_norm_w, rw_mu, rw_w0,
                                 rw_w_up, rw_a0, rw_a_up, rw_g_up, rw_k_k, rw_k_a, rw_r_k, rw_ln_w,
                                 rw_ln_b) for l in range(DEPTH)]
    vec = lambda g: g.reshape(DEPTH, 1, D_MODEL)
    ffn = {"g_pre_mix": vec(g_pre_mix), "g_post_mix": vec(g_post_mix), "g_pre_ffn": vec(g_pre_ffn),
           "g_post_ffn": vec(g_post_ffn), "w_out": w_out.astype(BF16), "w_gate": w_gate.astype(BF16),
           "w_up": w_up.astype(BF16), "w_down": w_down.astype(BF16)}

    zero_states = [(jnp.zeros((1, CONV_W - 1, DN_QKV), F32), jnp.zeros((1, 4, LANES, LANES), F32),
                    jnp.zeros((1, RW_COLS), F32), jnp.zeros((1, 4, LANES, LANES), F32))] * DEPTH
    _, meta_states = _trunk(meta.astype(F32), zero_states, layers, ffn,
                            batch=1, seq=N_META, chunk=N_META, n_pad=0)

    yp, ps = _trunk(x_prompt.reshape(batch * seq, D_MODEL), meta_states, layers, ffn,
                    batch=batch, seq=seq, chunk=PROMPT_CHUNK, n_pad=0)

    n_pad = SUBLANES - dec_seq
    sample_states = [(state_dn_conv[l], state_dn[l], state_rw_shift[l], _rw_to_pairs(state_rw[l]))
                     for l in range(DEPTH)]
    ys, ss = _trunk(x_sample.reshape(dec_batch * dec_seq, D_MODEL), sample_states, layers, ffn,
                    batch=dec_batch, seq=SUBLANES, chunk=SUBLANES, n_pad=n_pad)

    def collect(sts):
        return (jnp.stack([s[0] for s in sts]), jnp.stack([s[1] for s in sts]),
                jnp.stack([s[2] for s in sts]), jnp.stack([_rw_from_pairs(s[3]) for s in sts]))

    cp, dp, sp, rp = collect(ps)
    cs, ds, sh, rs = collect(ss)
    return (yp.reshape(batch, seq, D_MODEL), ys.reshape(dec_batch, dec_seq, D_MODEL),
            cp, dp, sp, rp, cs, ds, sh, rs)
```

```python
import functools
import math

import jax
import jax.numpy as jnp
from jax import lax
from jax.experimental import pallas as pl
from jax.experimental.pallas import tpu as pltpu

F32 = jnp.float32
BF16 = jnp.bfloat16

D_MODEL = 1024
DEPTH = 4
N_META = 16
DN_HEADS = 4
DN_HEAD_DIM = 128
DN_WIDTH = DN_HEADS * DN_HEAD_DIM
DN_QKV = 3 * DN_WIDTH
RW_HEADS = 8
RW_HEAD_DIM = 64
RW_WIDTH = RW_HEADS * RW_HEAD_DIM
W_LORA = 64
A_LORA = 64
G_LORA = 128
RW_COLS = 3 * RW_WIDTH + W_LORA + A_LORA + G_LORA
D_FF = 2816
CONV_W = 4
RMS_EPS = 1e-6
L2_EPS = 1e-6
GN_EPS = 64e-5

LANES = 128
SUBLANES = 8
VMEM_LIMIT = 56 * 1024 * 1024

C_QKV = 0
C_Z = DN_QKV
C_BG = C_Z + DN_WIDTH
C_RW = C_BG + LANES
P_COLS = C_RW + RW_COLS
GROUP = 64
PROMPT_ROWS = 256


def _sigmoid(x):
    return 1.0 / (1.0 + jnp.exp(-x))


def _softplus(x):
    return jnp.maximum(x, 0.0) + jnp.log(1.0 + jnp.exp(-jnp.abs(x)))


def _rms(x, g):
    return x * lax.rsqrt(jnp.mean(x * x, axis=-1, keepdims=True) + RMS_EPS) * g


def _mm(a, b):
    return jnp.dot(a.astype(BF16), b.astype(BF16), preferred_element_type=F32)


def _mm_nt(a, b):
    return lax.dot_general(a.astype(BF16), b.astype(BF16), (((1,), (1,)), ((), ())),
                           preferred_element_type=F32)


def _mm_tn(a, b):
    return lax.dot_general(a.astype(BF16), b.astype(BF16), (((0,), (0,)), ((), ())),
                           preferred_element_type=F32)


def _mm_sel(sel, x):
    sel = sel.astype(BF16)
    x1 = x.astype(BF16)
    r1 = x - x1.astype(F32)
    x2 = r1.astype(BF16)
    x3 = (r1 - x2.astype(F32)).astype(BF16)
    dot = lambda t: jnp.dot(sel, t, preferred_element_type=F32)
    return dot(x1) + dot(x2) + dot(x3)


def _bdot(a, b):
    return jnp.dot(a.astype(BF16), b, preferred_element_type=F32)


def _inproj_body(x_ref, g_ref, w_ref, o_ref):
    h = _rms(x_ref[...], g_ref[...])
    o_ref[...] = _bdot(h, w_ref[...])


def _inproj(x, g, w):
    rows = x.shape[0]
    tm = min(rows, 256)
    return pl.pallas_call(
        _inproj_body,
        grid=(rows // tm,),
        in_specs=[
            pl.BlockSpec((tm, D_MODEL), lambda i: (i, 0)),
            pl.BlockSpec((1, D_MODEL), lambda i: (0, 0)),
            pl.BlockSpec((D_MODEL, P_COLS), lambda i: (0, 0)),
        ],
        out_specs=pl.BlockSpec((tm, P_COLS), lambda i: (i, 0)),
        out_shape=jax.ShapeDtypeStruct((rows, P_COLS), F32),
        compiler_params=pltpu.CompilerParams(
            dimension_semantics=("parallel",), vmem_limit_bytes=VMEM_LIMIT),
        name="inproj",
    )(x, g, w)


FF_CHUNK = D_FF // 2


def _post_body(x_ref, m_ref, wo_ref, g1_ref, g2_ref, wg_ref, wu_ref, wd_ref, g3_ref, o_ref):
    mixed = _bdot(m_ref[...], wo_ref[...])
    x1 = x_ref[...] + _rms(mixed, g1_ref[...])
    h = _rms(x1, g2_ref[...]).astype(BF16)
    acc = jnp.zeros(x1.shape, F32)
    for c in range(D_FF // FF_CHUNK):
        sl = slice(c * FF_CHUNK, (c + 1) * FF_CHUNK)
        gate = jnp.dot(h, wg_ref[:, sl], preferred_element_type=F32)
        up = jnp.dot(h, wu_ref[:, sl], preferred_element_type=F32)
        act = gate * _sigmoid(gate) * up
        acc = acc + _bdot(act, wd_ref[sl, :])
    o_ref[...] = x1 + _rms(acc, g3_ref[...])


def _post(x, mixed, wo, g1, g2, wg, wu, wd, g3):
    rows = x.shape[0]
    tm = min(rows, 512)
    row_spec = pl.BlockSpec((tm, D_MODEL), lambda i: (i, 0))
    vec_spec = pl.BlockSpec((1, D_MODEL), lambda i: (0, 0))

    def resident(shape):
        return pl.BlockSpec(shape, lambda i: (0, 0), pipeline_mode=pl.Buffered(1))

    return pl.pallas_call(
        _post_body,
        grid=(rows // tm,),
        in_specs=[row_spec, row_spec, resident((D_MODEL, D_MODEL)), vec_spec, vec_spec,
                  resident((D_MODEL, D_FF)), resident((D_MODEL, D_FF)), resident((D_FF, D_MODEL)),
                  vec_spec],
        out_specs=row_spec,
        out_shape=jax.ShapeDtypeStruct((rows, D_MODEL), F32),
        compiler_params=pltpu.CompilerParams(
            dimension_semantics=("parallel",), vmem_limit_bytes=VMEM_LIMIT),
        name="post",
    )(x, mixed, wo, g1, g2, wg, wu, wd, g3)


def _inv_unit_lower_many(mats, eye, chunk):
    ns = [-a for a in mats]
    ps = [eye + n for n in ns]
    for _ in range(max(int(math.ceil(math.log2(chunk))) - 1, 0)):
        ns = [_mm(n, n) for n in ns]
        ps = [p + _mm(p, n) for p, n in zip(ps, ns)]
    return ps


def _stack(*parts):
    return jnp.concatenate(parts, axis=0)


def _mixer_body(proj_ref, conv0_ref, shift0_ref, dn0_ref, rw0_ref, convw_ref, gdnp_ref, dnnorm_ref,
                mu_ref, rwvec_ref, wup_ref, aup_ref, gup_ref,
                mixed_ref, dn_out_ref, rw_out_ref,
                ext_qkv, ext_rw, dn_s, rw_s, *, rows, group, chunk, n_pad, carry):
    R, GS, C = rows, group, chunk
    NG = R // GS
    NS = GS // C
    S2 = 2 * GS
    H = SUBLANES
    shift = int(math.log2(C))
    t = pl.program_id(1)

    if carry:
        @pl.when(t == 0)
        def _():
            ext_qkv[0:H, :] = conv0_ref[0]
            ext_rw[0:H, :] = shift0_ref[0]
            dn_s[...] = dn0_ref[0]
            rw_s[...] = rw0_ref[0]
    else:
        ext_qkv[0:H, :] = jnp.zeros((H, DN_QKV), F32)
        ext_rw[0:H, :] = jnp.zeros((H, RW_COLS), F32)

    ri = lax.broadcasted_iota(jnp.int32, (S2, S2), 0)
    ci = lax.broadcasted_iota(jnp.int32, (S2, S2), 1)
    same = (ri >> shift) == (ci >> shift)
    strict = same & (ci < ri)
    incl = same & (ci <= ri)
    eye2 = (ri == ci).astype(F32)
    ones2 = jnp.ones((S2, S2), F32)
    r1 = lax.broadcasted_iota(jnp.int32, (GS, GS), 0)
    c1 = lax.broadcasted_iota(jnp.int32, (GS, GS), 1)
    lt_blk = (((r1 >> shift) == (c1 >> shift)) & (c1 <= r1)).astype(F32)
    last_sel = (c1 == (((r1 >> shift) << shift) + (C - 1))).astype(F32)
    lane = lax.broadcasted_iota(jnp.int32, (1, LANES), 1)
    m0 = (lane < RW_HEAD_DIM).astype(F32)
    m1 = 1.0 - m0
    bi = lax.broadcasted_iota(jnp.int32, (LANES, LANES), 0)
    bj = lax.broadcasted_iota(jnp.int32, (LANES, LANES), 1)
    bd = ((bi >= RW_HEAD_DIM) == (bj >= RW_HEAD_DIM)).astype(F32)
    if n_pad:
        rowm = ((lax.broadcasted_iota(jnp.int32, (R, 1), 0) & (C - 1)) >= n_pad).astype(F32)
    else:
        rowm = None
    grp = [slice(g * GS, (g + 1) * GS) for g in range(NG)]

    def block_last(x):
        if NS == 1:
            return jnp.broadcast_to(x[GS - 1:GS, :], x.shape)
        return _mm_sel(last_sel, x)

    x_qkv = proj_ref[:, C_QKV:C_QKV + DN_QKV]
    ext_qkv[H:H + R, :] = x_qkv
    cw = convw_ref[...]
    conv = (ext_qkv[H - 3:H - 3 + R, :] * cw[0:1, :] + ext_qkv[H - 2:H - 2 + R, :] * cw[1:2, :]
            + ext_qkv[H - 1:H - 1 + R, :] * cw[2:3, :] + x_qkv * cw[3:4, :])
    qkv = conv * _sigmoid(conv)
    if carry:
        ext_qkv[0:H, :] = ext_qkv[R:R + H, :]

    z = proj_ref[:, C_Z:C_Z + DN_WIDTH]
    bg = proj_ref[:, C_BG:C_BG + LANES]
    gp = gdnp_ref[...]
    beta_all = _sigmoid(bg)
    glog_all = -jnp.exp(gp[1:2, :]) * _softplus(bg + gp[0:1, :])
    if rowm is not None:
        beta_all = beta_all * rowm
        glog_all = glog_all * rowm
    gcum = [_mm_sel(lt_blk, glog_all[rs]) for rs in grp]
    glast = [block_last(gc) for gc in gcum]

    def l2n(x):
        return x * lax.rsqrt(jnp.sum(x * x, axis=-1, keepdims=True) + L2_EPS)

    hl = lambda base, h: slice(base + h * LANES, base + (h + 1) * LANES)
    qn = [l2n(qkv[:, hl(0, h)]) * (DN_HEAD_DIM ** -0.5) for h in range(DN_HEADS)]
    kn = [l2n(qkv[:, hl(DN_WIDTH, h)]) for h in range(DN_HEADS)]
    vv = [qkv[:, hl(2 * DN_WIDTH, h)] for h in range(DN_HEADS)]

    cols = proj_ref[:, C_RW:C_RW + RW_COLS]
    ext_rw[H:H + R, :] = cols
    prev = ext_rw[H - 1:H - 1 + R, :]
    if carry:
        ext_rw[0:H, :] = ext_rw[R:R + H, :]
    xs = cols + (prev - cols) * mu_ref[...]
    r = xs[:, 0:RW_WIDTH]
    kr = xs[:, RW_WIDTH:2 * RW_WIDTH]
    vr = xs[:, 2 * RW_WIDTH:3 * RW_WIDTH]
    wa = xs[:, 3 * RW_WIDTH:3 * RW_WIDTH + LANES]
    gd = xs[:, 3 * RW_WIDTH + LANES:RW_COLS]
    rv = rwvec_ref[...]
    w0, a0, k_k, k_a, r_k, ln_w, ln_b = (rv[i:i + 1, :] for i in range(7))
    w_log = -_softplus(-(w0 + _mm(jnp.tanh(wa), wup_ref[...]))) - 0.5
    lw = -jnp.exp(w_log)
    a = _sigmoid(a0 + _mm(wa, aup_ref[...]))
    g = _mm(_sigmoid(gd), gup_ref[...])
    kk_raw = kr * k_k
    kr2 = kr * (1.0 + (a - 1.0) * k_a)
    bonus_in = r * kr2 * r_k
    if rowm is not None:
        lw = lw * rowm
        kr2 = kr2 * rowm
    cl_g = [_mm_sel(lt_blk, lw[rs]) for rs in grp]
    cl = cl_g[0] if NG == 1 else jnp.concatenate(cl_g, axis=0)
    p_last = [jnp.exp(block_last(c)) for c in cl_g]
    e_cl = jnp.exp(cl)
    e_prev = jnp.exp(cl - lw)
    e_inv = jnp.exp(-cl)
    r_t = r * e_cl
    k_t = kr2 * e_inv
    pl_ = lambda p: slice(p * LANES, (p + 1) * LANES)
    kk = []
    for p in range(RW_HEADS // 2):
        kp = kk_raw[:, pl_(p)]
        kp = kp * lax.rsqrt(_mm(kp * kp, bd) + L2_EPS)
        if rowm is not None:
            kp = kp * rowm
        kk.append(kp)

    gd_keys = [(gi, p) for gi in range(NG) for p in range(DN_HEADS // 2)]

    def gstack(arrs, gi, p):
        return _stack(arrs[2 * p][grp[gi]], arrs[2 * p + 1][grp[gi]])

    def gcol(mat, col, p):
        return _stack(mat[:, col + 2 * p:col + 2 * p + 1], mat[:, col + 2 * p + 1:col + 2 * p + 2])

    q2 = [gstack(qn, gi, p) for gi, p in gd_keys]
    k2 = [gstack(kn, gi, p) for gi, p in gd_keys]
    v2 = [gstack(vv, gi, p) for gi, p in gd_keys]
    beta2 = [gcol(beta_all[grp[gi]], 0, p) for gi, p in gd_keys]
    gc2 = [gcol(gcum[gi], DN_HEADS, p) for gi, p in gd_keys]
    gl2 = [gcol(glast[gi], DN_HEADS, p) for gi, p in gd_keys]
    rb = [_mm_sel(ones2, eye2 * c) for c in gc2]
    dec = [jnp.exp(jnp.where(incl, c - b, 0.0)) for c, b in zip(gc2, rb)]
    kb = [k * b for k, b in zip(k2, beta2)]
    s_kk = [_mm_nt(x, k) for x, k in zip(kb, k2)]
    s_qk = [_mm_nt(x, k) for x, k in zip(q2, k2)]
    a_mat = [jnp.where(strict, s * d, 0.0) for s, d in zip(s_kk, dec)]
    qk = [jnp.where(incl, s * d, 0.0) for s, d in zip(s_qk, dec)]
    t_dn = _inv_unit_lower_many(a_mat, eye2, C)
    eg = [jnp.exp(c) for c in gc2]
    uw = [_mm(tm, jnp.concatenate([v * b, x * e], axis=1))
          for tm, v, b, x, e in zip(t_dn, v2, beta2, kb, eg)]
    qo = [_mm(m, x) for m, x in zip(qk, uw)]
    u_dn = [x[:, :LANES] for x in uw]
    w_dn = [x[:, LANES:] for x in uw]
    o2_dn = [x[:, :LANES] for x in qo]
    qg2 = [q * e - x[:, LANES:] for q, e, x in zip(q2, eg, qo)]
    k_dec = [k * jnp.exp(l - c) for k, l, c in zip(k2, gl2, gc2)]
    e_gl = [jnp.exp(l) for l in gl2]

    rw_keys = [(gi, p) for gi in range(NG) for p in range(RW_HEADS // 2)]
    kk_t = [kk[p][grp[gi]] * e_prev[grp[gi], pl_(p)] for gi, p in rw_keys]
    a_t = [kk[p][grp[gi]] * a[grp[gi], pl_(p)] * e_inv[grp[gi], pl_(p)] for gi, p in rw_keys]
    rt_p = [r_t[grp[gi], pl_(p)] for gi, p in rw_keys]
    kt_p = [k_t[grp[gi], pl_(p)] for gi, p in rw_keys]
    v_p = [vr[grp[gi], pl_(p)] for gi, p in rw_keys]
    l_kk = [_stack(x * m0, x * m1) for x in kk_t]
    l_r = [_stack(x * m0, x * m1) for x in rt_p]
    vs = [_stack(x * m0, x * m1) for x in v_p]
    if S2 % LANES == 0:
        sc = [_mm_nt(_stack(lk, lr), _stack(x, x, y, y))
              for lk, lr, x, y in zip(l_kk, l_r, a_t, kt_p)]
        m_ka = [jnp.where(strict, s[:S2, :S2], 0.0) for s in sc]
        m_kv = [jnp.where(strict, s[:S2, S2:], 0.0) for s in sc]
        a_ra = [jnp.where(incl, s[S2:, :S2], 0.0) for s in sc]
        a_rk = [jnp.where(incl, s[S2:, S2:], 0.0) for s in sc]
    else:
        ra2 = [_stack(x, x) for x in a_t]
        rk2 = [_stack(x, x) for x in kt_p]
        m_ka = [jnp.where(strict, _mm_nt(x, y), 0.0) for x, y in zip(l_kk, ra2)]
        m_kv = [jnp.where(strict, _mm_nt(x, y), 0.0) for x, y in zip(l_kk, rk2)]
        a_ra = [jnp.where(incl, _mm_nt(x, y), 0.0) for x, y in zip(l_r, ra2)]
        a_rk = [jnp.where(incl, _mm_nt(x, y), 0.0) for x, y in zip(l_r, rk2)]
    t_rw = _inv_unit_lower_many(m_ka, eye2, C)
    mv = [_mm(m, x) for m, x in zip(m_kv, vs)]
    tlw = [_mm(tm, jnp.concatenate([lk, x], axis=1)) for tm, lk, x in zip(t_rw, l_kk, mv)]
    aa = [_mm(m, x) for m, x in zip(a_ra, tlw)]
    av = [_mm(m, x) for m, x in zip(a_rk, vs)]
    tl = [x[:, :LANES] for x in tlw]
    w2 = [x[:, LANES:] for x in tlw]
    ql = [lr - x[:, :LANES] for lr, x in zip(l_r, aa)]
    y2 = [x - y[:, LANES:] for x, y in zip(av, aa)]

    o_rows = [[None] * NG for _ in range(DN_HEADS)]
    y_rows = [[None] * NG for _ in range(RW_HEADS // 2)]
    if carry:
        s_dn = [dn_s[h] for h in range(DN_HEADS)]
        s_rw = [rw_s[p] for p in range(RW_HEADS // 2)]
        for gi in range(NG):
            hsl = lambda e: slice(e * GS, (e + 1) * GS)
            dn_i = [(gd_keys.index((gi, h // 2)), h % 2) for h in range(DN_HEADS)]
            rw_i = [rw_keys.index((gi, p)) for p in range(RW_HEADS // 2)]
            ss_dn = [_mm(_stack(w_dn[i][hsl(e)], qg2[i][hsl(e)]), s_dn[h])
                     for h, (i, e) in enumerate(dn_i)]
            ss_rw = [_mm_nt(_stack(tl[i], ql[i]), s_rw[p]) for p, i in enumerate(rw_i)]
            v_new = [u_dn[i][hsl(e)] - ss[:GS] for ss, (i, e) in zip(ss_dn, dn_i)]
            for h, (ss, (i, e)) in enumerate(zip(ss_dn, dn_i)):
                o_rows[h][gi] = ss[GS:] + o2_dn[i][hsl(e)]
            us = [ss[:S2] + w2[i] for ss, i in zip(ss_rw, rw_i)]
            for p, (ss, i) in enumerate(zip(ss_rw, rw_i)):
                ys = ss[S2:] + y2[i]
                y_rows[p][gi] = ys[:GS] + ys[GS:]
            u1 = [x[:GS] + x[GS:] for x in us]
            s_dn = [s * e_gl[i][e * GS:e * GS + 1, :] + _mm_tn(k_dec[i][hsl(e)], vn)
                    for s, vn, (i, e) in zip(s_dn, v_new, dn_i)]
            s_rw = [(s + _mm_tn(_stack(v_p[i], -x), _stack(kt_p[i], a_t[i]))) * bd
                    * p_last[gi][0:1, pl_(p)]
                    for p, (s, x, i) in enumerate(zip(s_rw, u1, rw_i))]
        for h in range(DN_HEADS):
            dn_s[h] = s_dn[h]
        for p in range(RW_HEADS // 2):
            rw_s[p] = s_rw[p]

        @pl.when(t == pl.num_programs(1) - 1)
        def _():
            dn_out_ref[0] = dn_s[...]
            rw_out_ref[0] = rw_s[...]
    else:
        for gi in range(NG):
            seqs = range(NS)
            rsl = lambda e, j: slice(e * GS + j * C, e * GS + (j + 1) * C)
            dn_items = [(j, h, gd_keys.index((gi, h // 2)), h % 2) for j in seqs for h in range(DN_HEADS)]
            rw_items = [(j, p, rw_keys.index((gi, p))) for j in seqs for p in range(RW_HEADS // 2)]
            s_dn = [dn0_ref[gi * NS + j, h] for j, h, _, _ in dn_items]
            s_rw = [rw0_ref[gi * NS + j, p] for j, p, _ in rw_items]
            ss_dn = [_mm(_stack(w_dn[i][rsl(e, j)], qg2[i][rsl(e, j)]), s)
                     for s, (j, h, i, e) in zip(s_dn, dn_items)]
            ss_rw = [_mm_nt(_stack(tl[i][rsl(0, j)], tl[i][rsl(1, j)], ql[i][rsl(0, j)], ql[i][rsl(1, j)]), s)
                     for s, (j, p, i) in zip(s_rw, rw_items)]
            v_new = [u_dn[i][rsl(e, j)] - ss[:C] for ss, (j, h, i, e) in zip(ss_dn, dn_items)]
            o_seq = [ss[C:] + o2_dn[i][rsl(e, j)] for ss, (j, h, i, e) in zip(ss_dn, dn_items)]
            us = [ss[:2 * C] + _stack(w2[i][rsl(0, j)], w2[i][rsl(1, j)]) for ss, (j, p, i) in zip(ss_rw, rw_items)]
            ysq = [ss[2 * C:] + _stack(y2[i][rsl(0, j)], y2[i][rsl(1, j)]) for ss, (j, p, i) in zip(ss_rw, rw_items)]
            u1 = [x[:C] + x[C:] for x in us]
            y_seq = [x[:C] + x[C:] for x in ysq]
            for s, vn, (j, h, i, e) in zip(s_dn, v_new, dn_items):
                row = e * GS + j * C
                dn_out_ref[gi * NS + j, h] = (s * e_gl[i][row:row + 1, :]
                                             + _mm_tn(k_dec[i][rsl(e, j)], vn))
            for s, x, (j, p, i) in zip(s_rw, u1, rw_items):
                sq = slice(j * C, (j + 1) * C)
                rw_out_ref[gi * NS + j, p] = ((s + _mm_tn(_stack(v_p[i][sq], -x), _stack(kt_p[i][sq], a_t[i][sq])))
                                              * bd * p_last[gi][j * C:j * C + 1, pl_(p)])
            for h in range(DN_HEADS):
                parts = [o for o, (j, hh, _, _) in zip(o_seq, dn_items) if hh == h]
                o_rows[h][gi] = parts[0] if NS == 1 else jnp.concatenate(parts, axis=0)
            for p in range(RW_HEADS // 2):
                parts = [y for y, (j, pp, _) in zip(y_seq, rw_items) if pp == p]
                y_rows[p][gi] = parts[0] if NS == 1 else jnp.concatenate(parts, axis=0)

    cat = lambda parts: parts[0] if len(parts) == 1 else jnp.concatenate(parts, axis=0)
    dn_w = dnnorm_ref[...]
    for h in range(DN_HEADS):
        zh = z[:, h * LANES:(h + 1) * LANES]
        mixed_ref[:, h * LANES:(h + 1) * LANES] = _rms(cat(o_rows[h]), dn_w) * (zh * _sigmoid(zh))
    for p in range(RW_HEADS // 2):
        sl = pl_(p)
        y = cat(y_rows[p])
        mu_g = _mm(y, bd) * (1.0 / RW_HEAD_DIM)
        yc = y - mu_g
        var = _mm(yc * yc, bd) * (1.0 / RW_HEAD_DIM)
        yn = yc * lax.rsqrt(var + GN_EPS) * ln_w[:, sl] + ln_b[:, sl]
        bonus = _mm(bonus_in[:, sl], bd) * vr[:, sl]
        mixed_ref[:, DN_WIDTH + p * LANES:DN_WIDTH + (p + 1) * LANES] = (yn + bonus) * g[:, sl]


def _mixer(proj, conv0, shift0, dn0, rw0, lp, *, batch, seq, rows, group, chunk, n_pad, carry):
    full = lambda shape: pl.BlockSpec(shape, lambda b, t: (0,) * len(shape))
    if carry:
        nt = seq // rows
        grid = (batch, nt)
        bsel = (lambda b: b) if conv0.shape[0] == batch else (lambda b: 0)
        row_map = lambda b, t: (b * nt + t, 0)
        st_in = pl.BlockSpec((1, 4, LANES, LANES), lambda b, t: (bsel(b), 0, 0, 0))
        st_out = pl.BlockSpec((1, 4, LANES, LANES), lambda b, t: (b, 0, 0, 0))
        halo_map = lambda b, t: (bsel(b), 0, 0)
    else:
        nseq = rows // chunk
        grid = (batch // nseq, 1)
        row_map = lambda b, t: (b, 0)
        st_in = pl.BlockSpec((nseq, 4, LANES, LANES), lambda b, t: (b, 0, 0, 0))
        st_out = st_in
        halo_map = lambda b, t: (0, 0, 0)
    st_shape = jax.ShapeDtypeStruct((batch, 4, LANES, LANES), F32)
    return pl.pallas_call(
        functools.partial(_mixer_body, rows=rows, group=group, chunk=chunk, n_pad=n_pad, carry=carry),
        grid=grid,
        in_specs=[
            pl.BlockSpec((rows, P_COLS), row_map),
            pl.BlockSpec((1, SUBLANES, DN_QKV), halo_map),
            pl.BlockSpec((1, SUBLANES, RW_COLS), halo_map),
            st_in, st_in,
            full((SUBLANES, DN_QKV)), full((SUBLANES, LANES)), full((1, LANES)),
            full((1, RW_COLS)), full((SUBLANES, RW_WIDTH)),
            full((LANES, RW_WIDTH)), full((LANES, RW_WIDTH)), full((G_LORA, RW_WIDTH)),
        ],
        out_specs=[pl.BlockSpec((rows, D_MODEL), row_map), st_out, st_out],
        out_shape=[jax.ShapeDtypeStruct((batch * seq, D_MODEL), F32), st_shape, st_shape],
        scratch_shapes=[
            pltpu.VMEM((SUBLANES + rows, DN_QKV), F32),
            pltpu.VMEM((SUBLANES + rows, RW_COLS), F32),
            pltpu.VMEM((4, LANES, LANES), F32),
            pltpu.VMEM((4, LANES, LANES), F32),
        ],
        compiler_params=pltpu.CompilerParams(
            dimension_semantics=("parallel", "arbitrary"), vmem_limit_bytes=VMEM_LIMIT),
        name="mixer",
    )(proj, conv0, shift0, dn0, rw0, lp["conv_w"], lp["gdnp"], lp["dn_norm"], lp["mu"],
      lp["rwvec"], lp["w_up"], lp["a_up"], lp["g_up"])


def _rw_to_pairs(s):
    b = s.shape[0]
    sp = s.reshape(b, RW_HEADS // 2, 2, RW_HEAD_DIM, RW_HEAD_DIM)
    zero = jnp.zeros_like(sp[:, :, 0])
    top = jnp.concatenate([sp[:, :, 0], zero], axis=-1)
    bot = jnp.concatenate([zero, sp[:, :, 1]], axis=-1)
    return jnp.concatenate([top, bot], axis=-2)


def _rw_from_pairs(sp):
    b = sp.shape[0]
    d = RW_HEAD_DIM
    return jnp.stack([sp[:, :, :d, :d], sp[:, :, d:, d:]], axis=2).reshape(b, RW_HEADS, d, d)


def _pad_rows_front(a, rows):
    return jnp.pad(a, ((0, 0), (rows - a.shape[1], 0), (0, 0)))


def _prep_layer_params(l, w_in, dn_conv_w, dn_a_log, dn_dt_bias, dn_norm_w, rw_mu, rw_w0, rw_w_up,
                       rw_a0, rw_a_up, rw_g_up, rw_k_k, rw_k_a, rw_r_k, rw_ln_w, rw_ln_b):
    off = DN_QKV + DN_WIDTH
    wi = w_in[l]
    w_in_p = jnp.concatenate(
        [wi[:, :off + 2 * DN_HEADS], jnp.zeros((D_MODEL, LANES - 2 * DN_HEADS), F32),
         wi[:, off + 2 * DN_HEADS:]], axis=1).astype(BF16)
    gdnp = jnp.zeros((SUBLANES, LANES), F32)
    gdnp = gdnp.at[0, DN_HEADS:2 * DN_HEADS].set(dn_dt_bias[l])
    gdnp = gdnp.at[1, DN_HEADS:2 * DN_HEADS].set(dn_a_log[l])
    rwvec = jnp.stack([rw_w0[l], rw_a0[l], rw_k_k[l], rw_k_a[l], rw_r_k[l].reshape(RW_WIDTH),
                       rw_ln_w[l], rw_ln_b[l], jnp.zeros((RW_WIDTH,), F32)])
    zeros_lora = jnp.zeros((W_LORA, RW_WIDTH), F32)
    return {
        "w_in": w_in_p,
        "conv_w": jnp.concatenate([dn_conv_w[l], jnp.zeros((SUBLANES - CONV_W, DN_QKV), F32)], axis=0),
        "gdnp": gdnp,
        "dn_norm": dn_norm_w[l].reshape(1, LANES),
        "mu": rw_mu[l].reshape(1, RW_COLS),
        "rwvec": rwvec,
        "w_up": jnp.concatenate([rw_w_up[l], zeros_lora], axis=0),
        "a_up": jnp.concatenate([zeros_lora, rw_a_up[l]], axis=0),
        "g_up": rw_g_up[l],
    }


def _trunk(x, states, layers, ffn, *, batch, seq, rows, group, chunk, n_pad, carry):
    seq_real = seq - n_pad
    new_states = []
    for l in range(DEPTH):
        lp = layers[l]
        conv_s, dn_s, shift_s, rw_s = states[l]
        proj = _inproj(x, ffn["g_pre_mix"][l], lp["w_in"])
        proj3 = proj.reshape(batch, seq_real, P_COLS)
        if n_pad:
            b0 = conv_s.shape[0]
            hist = jnp.zeros((b0, n_pad, P_COLS), F32)
            hist = hist.at[:, n_pad - (CONV_W - 1):, C_QKV:C_QKV + DN_QKV].set(conv_s)
            hist = hist.at[:, n_pad - 1, C_RW:].set(shift_s)
            mix_in = jnp.concatenate([hist, proj3], axis=1).reshape(batch * seq, P_COLS)
        else:
            mix_in = proj
        if carry:
            conv0 = _pad_rows_front(conv_s, SUBLANES)
            shift0 = _pad_rows_front(shift_s[:, None, :], SUBLANES)
        else:
            conv0 = jnp.zeros((1, SUBLANES, DN_QKV), F32)
            shift0 = jnp.zeros((1, SUBLANES, RW_COLS), F32)
        mixed, dn_new, rw_new = _mixer(mix_in, conv0, shift0, dn_s, rw_s, lp, batch=batch, seq=seq,
                                       rows=rows, group=group, chunk=chunk, n_pad=n_pad, carry=carry)
        if n_pad:
            mixed = mixed.reshape(batch, seq, D_MODEL)[:, n_pad:].reshape(batch * seq_real, D_MODEL)
        x = _post(x, mixed, ffn["w_out"][l], ffn["g_post_mix"][l], ffn["g_pre_ffn"][l],
                  ffn["w_gate"][l], ffn["w_up"][l], ffn["w_down"][l], ffn["g_post_ffn"][l])
        conv_new = proj3[:, seq_real - (CONV_W - 1):, C_QKV:C_QKV + DN_QKV]
        shift_new = proj3[:, seq_real - 1, C_RW:]
        new_states.append((conv_new, dn_new, shift_new, rw_new))
    return x, new_states


def kernel(x_prompt, x_sample, state_dn_conv, state_dn, state_rw_shift, state_rw, meta, g_pre_mix, g_post_mix, g_pre_ffn, g_post_ffn, w_in, dn_conv_w, dn_a_log, dn_dt_bias, dn_norm_w, rw_mu, rw_w0, rw_w_up, rw_a0, rw_a_up, rw_g_up, rw_k_k, rw_k_a, rw_r_k, rw_ln_w, rw_ln_b, w_out, w_gate, w_up, w_down):
    batch, seq, _ = x_prompt.shape
    dec_batch, dec_seq, _ = x_sample.shape
    layers = [_prep_layer_params(l, w_in, dn_conv_w, dn_a_log, dn_dt_bias, dn_norm_w, rw_mu, rw_w0,
                                 rw_w_up, rw_a0, rw_a_up, rw_g_up, rw_k_k, rw_k_a, rw_r_k, rw_ln_w,
                                 rw_ln_b) for l in range(DEPTH)]
    vec = lambda g: g.reshape(DEPTH, 1, D_MODEL)
    ffn = {"g_pre_mix": vec(g_pre_mix), "g_post_mix": vec(g_post_mix), "g_pre_ffn": vec(g_pre_ffn),
           "g_post_ffn": vec(g_post_ffn), "w_out": w_out.astype(BF16), "w_gate": w_gate.astype(BF16),
           "w_up": w_up.astype(BF16), "w_down": w_down.astype(BF16)}

    zero_states = [(jnp.zeros((1, CONV_W - 1, DN_QKV), F32), jnp.zeros((1, 4, LANES, LANES), F32),
                    jnp.zeros((1, RW_COLS), F32), jnp.zeros((1, 4, LANES, LANES), F32))] * DEPTH
    _, meta_states = _trunk(meta.astype(F32), zero_states, layers, ffn,
                            batch=1, seq=N_META, rows=N_META, group=N_META, chunk=N_META,
                            n_pad=0, carry=True)

    yp, ps = _trunk(x_prompt.reshape(batch * seq, D_MODEL), meta_states, layers, ffn,
                    batch=batch, seq=seq, rows=PROMPT_ROWS, group=GROUP, chunk=GROUP, n_pad=0,
                    carry=True)

    n_pad = SUBLANES - dec_seq
    sample_states = [(state_dn_conv[l], state_dn[l], state_rw_shift[l], _rw_to_pairs(state_rw[l]))
                     for l in range(DEPTH)]
    ys, ss = _trunk(x_sample.reshape(dec_batch * dec_seq, D_MODEL), sample_states, layers, ffn,
                    batch=dec_batch, seq=SUBLANES, rows=GROUP, group=GROUP, chunk=SUBLANES,
                    n_pad=n_pad, carry=False)

    def collect(sts):
        return (jnp.stack([s[0] for s in sts]), jnp.stack([s[1] for s in sts]),
                jnp.stack([s[2] for s in sts]), jnp.stack([_rw_from_pairs(s[3]) for s in sts]))

    cp, dp, sp, rp = collect(ps)
    cs, ds, sh, rs = collect(ss)
    return (yp.reshape(batch, seq, D_MODEL), ys.reshape(dec_batch, dec_seq, D_MODEL),
            cp, dp, sp, rp, cs, ds, sh, rs)
```

```python
import functools
import math

import jax
import jax.numpy as jnp
from jax import lax
from jax.experimental import pallas as pl
from jax.experimental.pallas import tpu as pltpu

F32 = jnp.float32
BF16 = jnp.bfloat16

D_MODEL = 1024
DEPTH = 4
N_META = 16
DN_HEADS = 4
DN_HEAD_DIM = 128
DN_WIDTH = DN_HEADS * DN_HEAD_DIM
DN_QKV = 3 * DN_WIDTH
RW_HEADS = 8
RW_HEAD_DIM = 64
RW_WIDTH = RW_HEADS * RW_HEAD_DIM
W_LORA = 64
A_LORA = 64
G_LORA = 128
RW_COLS = 3 * RW_WIDTH + W_LORA + A_LORA + G_LORA
D_FF = 2816
CONV_W = 4
RMS_EPS = 1e-6
L2_EPS = 1e-6
GN_EPS = 64e-5

LANES = 128
SUBLANES = 8
VMEM_LIMIT = 56 * 1024 * 1024

C_QKV = 0
C_Z = DN_QKV
C_BG = C_Z + DN_WIDTH
C_RW = C_BG + LANES
P_COLS = C_RW + RW_COLS
GROUP = 64
PROMPT_ROWS = 256


def _sigmoid(x):
    return 1.0 / (1.0 + jnp.exp(-x))


def _softplus(x):
    return jnp.maximum(x, 0.0) + jnp.log(1.0 + jnp.exp(-jnp.abs(x)))


def _rms(x, g):
    return x * lax.rsqrt(jnp.mean(x * x, axis=-1, keepdims=True) + RMS_EPS) * g


def _mm(a, b):
    return jnp.dot(a.astype(BF16), b.astype(BF16), preferred_element_type=F32)


def _mm_nt(a, b):
    return lax.dot_general(a.astype(BF16), b.astype(BF16), (((1,), (1,)), ((), ())),
                           preferred_element_type=F32)


def _mm_tn(a, b):
    return lax.dot_general(a.astype(BF16), b.astype(BF16), (((0,), (0,)), ((), ())),
                           preferred_element_type=F32)


def _mm_sel(sel, x):
    sel = sel.astype(BF16)
    x1 = x.astype(BF16)
    x2 = (x - x1.astype(F32)).astype(BF16)
    dot = lambda t: jnp.dot(sel, t, preferred_element_type=F32)
    return dot(x1) + dot(x2)


def _bdot(a, b):
    return jnp.dot(a.astype(BF16), b, preferred_element_type=F32)


def _inproj_body(x_ref, g_ref, w_ref, o_ref):
    h = _rms(x_ref[...], g_ref[...])
    o_ref[...] = _bdot(h, w_ref[...])


def _inproj(x, g, w):
    rows = x.shape[0]
    tm = min(rows, 256)
    return pl.pallas_call(
        _inproj_body,
        grid=(rows // tm,),
        in_specs=[
            pl.BlockSpec((tm, D_MODEL), lambda i: (i, 0)),
            pl.BlockSpec((1, D_MODEL), lambda i: (0, 0)),
            pl.BlockSpec((D_MODEL, P_COLS), lambda i: (0, 0)),
        ],
        out_specs=pl.BlockSpec((tm, P_COLS), lambda i: (i, 0)),
        out_shape=jax.ShapeDtypeStruct((rows, P_COLS), F32),
        compiler_params=pltpu.CompilerParams(
            dimension_semantics=("parallel",), vmem_limit_bytes=VMEM_LIMIT),
        name="inproj",
    )(x, g, w)


def _row_prep(proj, halo_qkv, halo_rw, cw, mu, rows):
    H = SUBLANES
    R = rows
    x_qkv = proj[:, C_QKV:C_QKV + DN_QKV]
    ext = jnp.concatenate([halo_qkv, x_qkv], axis=0)
    back = lambda a, k: pltpu.roll(a, k, 0)[H:H + R, :]
    conv = (back(ext, 3) * cw[0:1, :] + back(ext, 2) * cw[1:2, :] + back(ext, 1) * cw[2:3, :]
            + x_qkv * cw[3:4, :])
    qkv = conv * _sigmoid(conv)

    def l2n(x):
        return x * lax.rsqrt(jnp.sum(x * x, axis=-1, keepdims=True) + L2_EPS)

    hl = lambda base, h: slice(base + h * LANES, base + (h + 1) * LANES)
    parts = ([l2n(qkv[:, hl(0, h)]) * (DN_HEAD_DIM ** -0.5) for h in range(DN_HEADS)]
             + [l2n(qkv[:, hl(DN_WIDTH, h)]) for h in range(DN_HEADS)]
             + [qkv[:, hl(2 * DN_WIDTH, h)] for h in range(DN_HEADS)])
    cols = proj[:, C_RW:C_RW + RW_COLS]
    prev = back(jnp.concatenate([halo_rw, cols], axis=0), 1)
    xs = cols + (prev - cols) * mu
    return parts, xs, x_qkv[R - H:R, :], cols[R - H:R, :]


def _inproj_prep_body(x_ref, g_ref, w_ref, conv0_ref, shift0_ref, cw_ref, mu_ref,
                      o_ref, ctail_ref, stail_ref, halo_qkv, halo_rw, *, rows):
    @pl.when(pl.program_id(1) == 0)
    def _():
        halo_qkv[...] = conv0_ref[0]
        halo_rw[...] = shift0_ref[0]

    proj = _bdot(_rms(x_ref[...], g_ref[...]), w_ref[...])
    parts, xs, tail_qkv, tail_rw = _row_prep(proj, halo_qkv[...], halo_rw[...], cw_ref[...],
                                             mu_ref[...], rows)
    for idx, part in enumerate(parts):
        o_ref[:, idx * LANES:(idx + 1) * LANES] = part
    o_ref[:, C_Z:C_RW] = proj[:, C_Z:C_RW]
    o_ref[:, C_RW:C_RW + RW_COLS] = xs
    halo_qkv[...] = tail_qkv
    halo_rw[...] = tail_rw
    ctail_ref[0] = tail_qkv
    stail_ref[0] = tail_rw


def _inproj_prep(x, g, w, conv0, shift0, lp, *, batch, seq, rows):
    nt = seq // rows
    bsel = (lambda b: b) if conv0.shape[0] == batch else (lambda b: 0)
    full = lambda shape: pl.BlockSpec(shape, lambda b, t: (0,) * len(shape))
    return pl.pallas_call(
        functools.partial(_inproj_prep_body, rows=rows),
        grid=(batch, nt),
        in_specs=[
            pl.BlockSpec((rows, D_MODEL), lambda b, t: (b * nt + t, 0)),
            full((1, D_MODEL)), full((D_MODEL, P_COLS)),
            pl.BlockSpec((1, SUBLANES, DN_QKV), lambda b, t: (bsel(b), 0, 0)),
            pl.BlockSpec((1, SUBLANES, RW_COLS), lambda b, t: (bsel(b), 0, 0)),
            full((SUBLANES, DN_QKV)), full((1, RW_COLS)),
        ],
        out_specs=[
            pl.BlockSpec((rows, P_COLS), lambda b, t: (b * nt + t, 0)),
            pl.BlockSpec((1, SUBLANES, DN_QKV), lambda b, t: (b, 0, 0)),
            pl.BlockSpec((1, SUBLANES, RW_COLS), lambda b, t: (b, 0, 0)),
        ],
        out_shape=[jax.ShapeDtypeStruct((batch * seq, P_COLS), F32),
                   jax.ShapeDtypeStruct((batch, SUBLANES, DN_QKV), F32),
                   jax.ShapeDtypeStruct((batch, SUBLANES, RW_COLS), F32)],
        scratch_shapes=[pltpu.VMEM((SUBLANES, DN_QKV), F32), pltpu.VMEM((SUBLANES, RW_COLS), F32)],
        compiler_params=pltpu.CompilerParams(
            dimension_semantics=("parallel", "arbitrary"), vmem_limit_bytes=VMEM_LIMIT),
        name="inproj_prep",
    )(x, g, w, conv0, shift0, lp["conv_w"], lp["mu"])


FF_CHUNK = D_FF // 2


def _post_body(x_ref, m_ref, wo_ref, g1_ref, g2_ref, wg_ref, wu_ref, wd_ref, g3_ref, o_ref):
    mixed = _bdot(m_ref[...], wo_ref[...])
    x1 = x_ref[...] + _rms(mixed, g1_ref[...])
    h = _rms(x1, g2_ref[...]).astype(BF16)
    acc = jnp.zeros(x1.shape, F32)
    for c in range(D_FF // FF_CHUNK):
        sl = slice(c * FF_CHUNK, (c + 1) * FF_CHUNK)
        gate = jnp.dot(h, wg_ref[:, sl], preferred_element_type=F32)
        up = jnp.dot(h, wu_ref[:, sl], preferred_element_type=F32)
        act = gate * _sigmoid(gate) * up
        acc = acc + _bdot(act, wd_ref[sl, :])
    o_ref[...] = x1 + _rms(acc, g3_ref[...])


def _post(x, mixed, wo, g1, g2, wg, wu, wd, g3):
    rows = x.shape[0]
    tm = min(rows, 512)
    row_spec = pl.BlockSpec((tm, D_MODEL), lambda i: (i, 0))
    vec_spec = pl.BlockSpec((1, D_MODEL), lambda i: (0, 0))

    def resident(shape):
        return pl.BlockSpec(shape, lambda i: (0, 0), pipeline_mode=pl.Buffered(1))

    return pl.pallas_call(
        _post_body,
        grid=(rows // tm,),
        in_specs=[row_spec, row_spec, resident((D_MODEL, D_MODEL)), vec_spec, vec_spec,
                  resident((D_MODEL, D_FF)), resident((D_MODEL, D_FF)), resident((D_FF, D_MODEL)),
                  vec_spec],
        out_specs=row_spec,
        out_shape=jax.ShapeDtypeStruct((rows, D_MODEL), F32),
        compiler_params=pltpu.CompilerParams(
            dimension_semantics=("parallel",), vmem_limit_bytes=VMEM_LIMIT),
        name="post",
    )(x, mixed, wo, g1, g2, wg, wu, wd, g3)


def _inv_unit_lower_many(mats, eye, chunk):
    ns = [-a for a in mats]
    ps = [eye + n for n in ns]
    for _ in range(max(int(math.ceil(math.log2(chunk))) - 1, 0)):
        ns = [_mm(n, n) for n in ns]
        ps = [p + _mm(p, n) for p, n in zip(ps, ns)]
    return ps


def _stack(*parts):
    return jnp.concatenate(parts, axis=0)


def _pair_blockdiag(s0, s1):
    zero = jnp.zeros_like(s0)
    return jnp.concatenate([jnp.concatenate([s0, zero], axis=1),
                            jnp.concatenate([zero, s1], axis=1)], axis=0)


def _mixer_body(proj_ref, dn0_ref, rw0_ref, convw_ref, gdnp_ref, dnnorm_ref,
                mu_ref, rwvec_ref, wup_ref, aup_ref, gup_ref,
                mixed_ref, dn_out_ref, rw_out_ref, *scratch, rows, group, chunk, n_pad, carry):
    R, GS, C = rows, group, chunk
    NG = R // GS
    NS = GS // C
    S2 = 2 * GS
    H = SUBLANES
    shift = int(math.log2(C))
    t = pl.program_id(1)

    D = RW_HEAD_DIM
    if carry:
        dn_s, rw_s = scratch

        @pl.when(t == 0)
        def _():
            dn_s[...] = dn0_ref[0]
            for p in range(RW_HEADS // 2):
                rw_s[p] = _pair_blockdiag(rw0_ref[0, 2 * p], rw0_ref[0, 2 * p + 1])

    ri = lax.broadcasted_iota(jnp.int32, (S2, S2), 0)
    ci = lax.broadcasted_iota(jnp.int32, (S2, S2), 1)
    same = (ri >> shift) == (ci >> shift)
    strict = same & (ci < ri)
    incl = same & (ci <= ri)
    eye2 = (ri == ci).astype(F32)
    ones2 = jnp.ones((S2, S2), F32)
    r1 = lax.broadcasted_iota(jnp.int32, (GS, GS), 0)
    c1 = lax.broadcasted_iota(jnp.int32, (GS, GS), 1)
    lt_blk = (((r1 >> shift) == (c1 >> shift)) & (c1 <= r1)).astype(F32)
    last_sel = (c1 == (((r1 >> shift) << shift) + (C - 1))).astype(F32)
    lane = lax.broadcasted_iota(jnp.int32, (1, LANES), 1)
    m0 = (lane < RW_HEAD_DIM).astype(F32)
    m1 = 1.0 - m0
    bi = lax.broadcasted_iota(jnp.int32, (LANES, LANES), 0)
    bj = lax.broadcasted_iota(jnp.int32, (LANES, LANES), 1)
    bd = ((bi >= RW_HEAD_DIM) == (bj >= RW_HEAD_DIM)).astype(F32)
    if n_pad:
        rowm = ((lax.broadcasted_iota(jnp.int32, (R, 1), 0) & (C - 1)) >= n_pad).astype(F32)
    else:
        rowm = None
    grp = [slice(g * GS, (g + 1) * GS) for g in range(NG)]

    def block_last(x):
        if NS == 1:
            return jnp.broadcast_to(x[GS - 1:GS, :], x.shape)
        return _mm_sel(last_sel, x)

    hl = lambda base, h: slice(base + h * LANES, base + (h + 1) * LANES)
    if carry:
        qn = [proj_ref[:, hl(0, h)] for h in range(DN_HEADS)]
        kn = [proj_ref[:, hl(DN_WIDTH, h)] for h in range(DN_HEADS)]
        vv = [proj_ref[:, hl(2 * DN_WIDTH, h)] for h in range(DN_HEADS)]
        xs = proj_ref[:, C_RW:C_RW + RW_COLS]
    else:
        parts, xs, _, _ = _row_prep(proj_ref, jnp.zeros((H, DN_QKV), F32), jnp.zeros((H, RW_COLS), F32),
                                    convw_ref[...], mu_ref[...], R)
        qn, kn, vv = parts[:DN_HEADS], parts[DN_HEADS:2 * DN_HEADS], parts[2 * DN_HEADS:]

    z = proj_ref[:, C_Z:C_Z + DN_WIDTH]
    bg = proj_ref[:, C_BG:C_BG + LANES]
    gp = gdnp_ref[...]
    beta_all = _sigmoid(bg)
    glog_all = -jnp.exp(gp[1:2, :]) * _softplus(bg + gp[0:1, :])
    if rowm is not None:
        beta_all = beta_all * rowm
        glog_all = glog_all * rowm
    gcum = [_mm_sel(lt_blk, glog_all[rs]) for rs in grp]
    glast = [block_last(gc) for gc in gcum]

    r = xs[:, 0:RW_WIDTH]
    kr = xs[:, RW_WIDTH:2 * RW_WIDTH]
    vr = xs[:, 2 * RW_WIDTH:3 * RW_WIDTH]
    wa = xs[:, 3 * RW_WIDTH:3 * RW_WIDTH + LANES]
    gd = xs[:, 3 * RW_WIDTH + LANES:RW_COLS]
    rv = rwvec_ref[...]
    w0, a0, k_k, k_a, r_k, ln_w, ln_b = (rv[i:i + 1, :] for i in range(7))
    w_log = -_softplus(-(w0 + _mm(jnp.tanh(wa), wup_ref[...]))) - 0.5
    lw = -jnp.exp(w_log)
    a = _sigmoid(a0 + _mm(wa, aup_ref[...]))
    g = _mm(_sigmoid(gd), gup_ref[...])
    kk_raw = kr * k_k
    kr2 = kr * (1.0 + (a - 1.0) * k_a)
    bonus_in = r * kr2 * r_k
    if rowm is not None:
        lw = lw * rowm
        kr2 = kr2 * rowm
    cl_g = [_mm_sel(lt_blk, lw[rs]) for rs in grp]
    cl = cl_g[0] if NG == 1 else jnp.concatenate(cl_g, axis=0)
    p_last = [jnp.exp(block_last(c)) for c in cl_g]
    e_cl = jnp.exp(cl)
    e_prev = jnp.exp(cl - lw)
    e_inv = jnp.exp(-cl)
    r_t = r * e_cl
    k_t = kr2 * e_inv
    pl_ = lambda p: slice(p * LANES, (p + 1) * LANES)
    kk = []
    for p in range(RW_HEADS // 2):
        kp = kk_raw[:, pl_(p)]
        kp = kp * lax.rsqrt(_mm(kp * kp, bd) + L2_EPS)
        if rowm is not None:
            kp = kp * rowm
        kk.append(kp)

    gd_keys = [(gi, p) for gi in range(NG) for p in range(DN_HEADS // 2)]

    def gstack(arrs, gi, p):
        return _stack(arrs[2 * p][grp[gi]], arrs[2 * p + 1][grp[gi]])

    def gcol(mat, col, p):
        return _stack(mat[:, col + 2 * p:col + 2 * p + 1], mat[:, col + 2 * p + 1:col + 2 * p + 2])

    q2 = [gstack(qn, gi, p) for gi, p in gd_keys]
    k2 = [gstack(kn, gi, p) for gi, p in gd_keys]
    v2 = [gstack(vv, gi, p) for gi, p in gd_keys]
    beta2 = [gcol(beta_all[grp[gi]], 0, p) for gi, p in gd_keys]
    gc2 = [gcol(gcum[gi], DN_HEADS, p) for gi, p in gd_keys]
    gl2 = [gcol(glast[gi], DN_HEADS, p) for gi, p in gd_keys]
    rb = [_mm_sel(ones2, eye2 * c) for c in gc2]
    dec = [jnp.exp(jnp.where(incl, c - b, 0.0)) for c, b in zip(gc2, rb)]
    kb = [k * b for k, b in zip(k2, beta2)]
    s_kk = [_mm_nt(x, k) for x, k in zip(kb, k2)]
    s_qk = [_mm_nt(x, k) for x, k in zip(q2, k2)]
    a_mat = [jnp.where(strict, s * d, 0.0) for s, d in zip(s_kk, dec)]
    qk = [jnp.where(incl, s * d, 0.0) for s, d in zip(s_qk, dec)]
    t_dn = _inv_unit_lower_many(a_mat, eye2, C)
    eg = [jnp.exp(c) for c in gc2]
    uw = [_mm(tm, jnp.concatenate([v * b, x * e], axis=1))
          for tm, v, b, x, e in zip(t_dn, v2, beta2, kb, eg)]
    qo = [_mm(m, x) for m, x in zip(qk, uw)]
    u_dn = [x[:, :LANES] for x in uw]
    w_dn = [x[:, LANES:] for x in uw]
    o2_dn = [x[:, :LANES] for x in qo]
    qg2 = [q * e - x[:, LANES:] for q, e, x in zip(q2, eg, qo)]
    k_dec = [k * jnp.exp(l - c) for k, l, c in zip(k2, gl2, gc2)]
    e_gl = [jnp.exp(l) for l in gl2]

    rw_keys = [(gi, p) for gi in range(NG) for p in range(RW_HEADS // 2)]
    kk_t = [kk[p][grp[gi]] * e_prev[grp[gi], pl_(p)] for gi, p in rw_keys]
    a_t = [kk[p][grp[gi]] * a[grp[gi], pl_(p)] * e_inv[grp[gi], pl_(p)] for gi, p in rw_keys]
    rt_p = [r_t[grp[gi], pl_(p)] for gi, p in rw_keys]
    kt_p = [k_t[grp[gi], pl_(p)] for gi, p in rw_keys]
    v_p = [vr[grp[gi], pl_(p)] for gi, p in rw_keys]
    l_kk = [_stack(x * m0, x * m1) for x in kk_t]
    l_r = [_stack(x * m0, x * m1) for x in rt_p]
    vs = [_stack(x * m0, x * m1) for x in v_p]
    if S2 % LANES == 0:
        sc = [_mm_nt(_stack(lk, lr), _stack(x, x, y, y))
              for lk, lr, x, y in zip(l_kk, l_r, a_t, kt_p)]
        m_ka = [jnp.where(strict, s[:S2, :S2], 0.0) for s in sc]
        m_kv = [jnp.where(strict, s[:S2, S2:], 0.0) for s in sc]
        a_ra = [jnp.where(incl, s[S2:, :S2], 0.0) for s in sc]
        a_rk = [jnp.where(incl, s[S2:, S2:], 0.0) for s in sc]
    else:
        ra2 = [_stack(x, x) for x in a_t]
        rk2 = [_stack(x, x) for x in kt_p]
        m_ka = [jnp.where(strict, _mm_nt(x, y), 0.0) for x, y in zip(l_kk, ra2)]
        m_kv = [jnp.where(strict, _mm_nt(x, y), 0.0) for x, y in zip(l_kk, rk2)]
        a_ra = [jnp.where(incl, _mm_nt(x, y), 0.0) for x, y in zip(l_r, ra2)]
        a_rk = [jnp.where(incl, _mm_nt(x, y), 0.0) for x, y in zip(l_r, rk2)]
    t_rw = _inv_unit_lower_many(m_ka, eye2, C)
    mv = [_mm(m, x) for m, x in zip(m_kv, vs)]
    tlw = [_mm(tm, jnp.concatenate([lk, x], axis=1)) for tm, lk, x in zip(t_rw, l_kk, mv)]
    aa = [_mm(m, x) for m, x in zip(a_ra, tlw)]
    av = [_mm(m, x) for m, x in zip(a_rk, vs)]
    tl = [x[:, :LANES] for x in tlw]
    w2 = [x[:, LANES:] for x in tlw]
    ql = [lr - x[:, :LANES] for lr, x in zip(l_r, aa)]
    y2 = [x - y[:, LANES:] for x, y in zip(av, aa)]

    o_rows = [[None] * NG for _ in range(DN_HEADS)]
    y_rows = [[None] * NG for _ in range(RW_HEADS // 2)]
    if carry:
        s_dn = [dn_s[h] for h in range(DN_HEADS)]
        s_rw = [rw_s[p] for p in range(RW_HEADS // 2)]
        for gi in range(NG):
            hsl = lambda e: slice(e * GS, (e + 1) * GS)
            dn_i = [(gd_keys.index((gi, h // 2)), h % 2) for h in range(DN_HEADS)]
            rw_i = [rw_keys.index((gi, p)) for p in range(RW_HEADS // 2)]
            ss_dn = [_mm(_stack(w_dn[i][hsl(e)], qg2[i][hsl(e)]), s_dn[h])
                     for h, (i, e) in enumerate(dn_i)]
            ss_rw = [_mm_nt(_stack(tl[i], ql[i]), s_rw[p]) for p, i in enumerate(rw_i)]
            v_new = [u_dn[i][hsl(e)] - ss[:GS] for ss, (i, e) in zip(ss_dn, dn_i)]
            for h, (ss, (i, e)) in enumerate(zip(ss_dn, dn_i)):
                o_rows[h][gi] = ss[GS:] + o2_dn[i][hsl(e)]
            us = [ss[:S2] + w2[i] for ss, i in zip(ss_rw, rw_i)]
            for p, (ss, i) in enumerate(zip(ss_rw, rw_i)):
                ys = ss[S2:] + y2[i]
                y_rows[p][gi] = ys[:GS] + ys[GS:]
            u1 = [x[:GS] + x[GS:] for x in us]
            s_dn = [s * e_gl[i][e * GS:e * GS + 1, :] + _mm_tn(k_dec[i][hsl(e)], vn)
                    for s, vn, (i, e) in zip(s_dn, v_new, dn_i)]
            s_rw = [(s + _mm_tn(_stack(v_p[i], -x), _stack(kt_p[i], a_t[i]))) * bd
                    * p_last[gi][0:1, pl_(p)]
                    for p, (s, x, i) in enumerate(zip(s_rw, u1, rw_i))]
        for h in range(DN_HEADS):
            dn_s[h] = s_dn[h]
        for p in range(RW_HEADS // 2):
            rw_s[p] = s_rw[p]

        @pl.when(t == pl.num_programs(1) - 1)
        def _():
            dn_out_ref[0] = dn_s[...]
            for p in range(RW_HEADS // 2):
                rw_out_ref[0, 2 * p] = rw_s[p][:D, :D]
                rw_out_ref[0, 2 * p + 1] = rw_s[p][D:, D:]
    else:
        for gi in range(NG):
            seqs = range(NS)
            rsl = lambda e, j: slice(e * GS + j * C, e * GS + (j + 1) * C)
            dn_items = [(j, h, gd_keys.index((gi, h // 2)), h % 2) for j in seqs for h in range(DN_HEADS)]
            rw_items = [(j, p, rw_keys.index((gi, p))) for j in seqs for p in range(RW_HEADS // 2)]
            s_dn = [dn0_ref[gi * NS + j, h] for j, h, _, _ in dn_items]
            s_rw = [_pair_blockdiag(rw0_ref[gi * NS + j, 2 * p], rw0_ref[gi * NS + j, 2 * p + 1])
                    for j, p, _ in rw_items]
            ss_dn = [_mm(_stack(w_dn[i][rsl(e, j)], qg2[i][rsl(e, j)]), s)
                     for s, (j, h, i, e) in zip(s_dn, dn_items)]
            ss_rw = [_mm_nt(_stack(tl[i][rsl(0, j)], tl[i][rsl(1, j)], ql[i][rsl(0, j)], ql[i][rsl(1, j)]), s)
                     for s, (j, p, i) in zip(s_rw, rw_items)]
            v_new = [u_dn[i][rsl(e, j)] - ss[:C] for ss, (j, h, i, e) in zip(ss_dn, dn_items)]
            o_seq = [ss[C:] + o2_dn[i][rsl(e, j)] for ss, (j, h, i, e) in zip(ss_dn, dn_items)]
            us = [ss[:2 * C] + _stack(w2[i][rsl(0, j)], w2[i][rsl(1, j)]) for ss, (j, p, i) in zip(ss_rw, rw_items)]
            ysq = [ss[2 * C:] + _stack(y2[i][rsl(0, j)], y2[i][rsl(1, j)]) for ss, (j, p, i) in zip(ss_rw, rw_items)]
            u1 = [x[:C] + x[C:] for x in us]
            y_seq = [x[:C] + x[C:] for x in ysq]
            for s, vn, (j, h, i, e) in zip(s_dn, v_new, dn_items):
                row = e * GS + j * C
                dn_out_ref[gi * NS + j, h] = (s * e_gl[i][row:row + 1, :]
                                             + _mm_tn(k_dec[i][rsl(e, j)], vn))
            for s, x, (j, p, i) in zip(s_rw, u1, rw_items):
                sq = slice(j * C, (j + 1) * C)
                s_new = ((s + _mm_tn(_stack(v_p[i][sq], -x), _stack(kt_p[i][sq], a_t[i][sq])))
                         * bd * p_last[gi][j * C:j * C + 1, pl_(p)])
                rw_out_ref[gi * NS + j, 2 * p] = s_new[:D, :D]
                rw_out_ref[gi * NS + j, 2 * p + 1] = s_new[D:, D:]
            for h in range(DN_HEADS):
                parts = [o for o, (j, hh, _, _) in zip(o_seq, dn_items) if hh == h]
                o_rows[h][gi] = parts[0] if NS == 1 else jnp.concatenate(parts, axis=0)
            for p in range(RW_HEADS // 2):
                parts = [y for y, (j, pp, _) in zip(y_seq, rw_items) if pp == p]
                y_rows[p][gi] = parts[0] if NS == 1 else jnp.concatenate(parts, axis=0)

    cat = lambda parts: parts[0] if len(parts) == 1 else jnp.concatenate(parts, axis=0)
    dn_w = dnnorm_ref[...]
    for h in range(DN_HEADS):
        zh = z[:, h * LANES:(h + 1) * LANES]
        mixed_ref[:, h * LANES:(h + 1) * LANES] = _rms(cat(o_rows[h]), dn_w) * (zh * _sigmoid(zh))
    for p in range(RW_HEADS // 2):
        sl = pl_(p)
        y = cat(y_rows[p])
        mu_g = _mm(y, bd) * (1.0 / RW_HEAD_DIM)
        yc = y - mu_g
        var = _mm(yc * yc, bd) * (1.0 / RW_HEAD_DIM)
        yn = yc * lax.rsqrt(var + GN_EPS) * ln_w[:, sl] + ln_b[:, sl]
        bonus = _mm(bonus_in[:, sl], bd) * vr[:, sl]
        mixed_ref[:, DN_WIDTH + p * LANES:DN_WIDTH + (p + 1) * LANES] = (yn + bonus) * g[:, sl]


def _mixer(proj, dn0, rw0, lp, *, batch, seq, rows, group, chunk, n_pad, carry, layer=None):
    full = lambda shape: pl.BlockSpec(shape, lambda b, t: (0,) * len(shape))
    dn_blk = (DN_HEADS, DN_HEAD_DIM, DN_HEAD_DIM)
    rw_blk = (RW_HEADS, RW_HEAD_DIM, RW_HEAD_DIM)
    if carry:
        nt = seq // rows
        grid = (batch, nt)
        bsel = (lambda b: b) if dn0.shape[0] == batch else (lambda b: 0)
        row_map = lambda b, t: (b * nt + t, 0)
        st_in = [pl.BlockSpec((1,) + blk, lambda b, t: (bsel(b), 0, 0, 0)) for blk in (dn_blk, rw_blk)]
        st_out = [pl.BlockSpec((1,) + blk, lambda b, t: (b, 0, 0, 0)) for blk in (dn_blk, rw_blk)]
        scratch = [pltpu.VMEM((DN_HEADS, LANES, LANES), F32), pltpu.VMEM((RW_HEADS // 2, LANES, LANES), F32)]
    else:
        nseq = rows // chunk
        grid = (batch // nseq, 1)
        row_map = lambda b, t: (b, 0)
        st_in = [pl.BlockSpec((None, nseq) + blk, lambda b, t: (layer, b, 0, 0, 0)) for blk in (dn_blk, rw_blk)]
        st_out = [pl.BlockSpec((nseq,) + blk, lambda b, t: (b, 0, 0, 0)) for blk in (dn_blk, rw_blk)]
        scratch = []
    return pl.pallas_call(
        functools.partial(_mixer_body, rows=rows, group=group, chunk=chunk, n_pad=n_pad, carry=carry),
        grid=grid,
        in_specs=[
            pl.BlockSpec((rows, P_COLS), row_map),
            *st_in,
            full((SUBLANES, DN_QKV)), full((SUBLANES, LANES)), full((1, LANES)),
            full((1, RW_COLS)), full((SUBLANES, RW_WIDTH)),
            full((LANES, RW_WIDTH)), full((LANES, RW_WIDTH)), full((G_LORA, RW_WIDTH)),
        ],
        out_specs=[pl.BlockSpec((rows, D_MODEL), row_map), *st_out],
        out_shape=[jax.ShapeDtypeStruct((batch * seq, D_MODEL), F32),
                   jax.ShapeDtypeStruct((batch,) + dn_blk, F32),
                   jax.ShapeDtypeStruct((batch,) + rw_blk, F32)],
        scratch_shapes=scratch,
        compiler_params=pltpu.CompilerParams(
            dimension_semantics=("parallel", "arbitrary"), vmem_limit_bytes=VMEM_LIMIT),
        name="mixer",
    )(proj, dn0, rw0, lp["conv_w"], lp["gdnp"], lp["dn_norm"], lp["mu"],
      lp["rwvec"], lp["w_up"], lp["a_up"], lp["g_up"])


def _prep_layer_params(l, w_in, dn_conv_w, dn_a_log, dn_dt_bias, dn_norm_w, rw_mu, rw_w0, rw_w_up,
                       rw_a0, rw_a_up, rw_g_up, rw_k_k, rw_k_a, rw_r_k, rw_ln_w, rw_ln_b):
    off = DN_QKV + DN_WIDTH
    wi = w_in[l]
    w_in_p = jnp.concatenate(
        [wi[:, :off + 2 * DN_HEADS], jnp.zeros((D_MODEL, LANES - 2 * DN_HEADS), F32),
         wi[:, off + 2 * DN_HEADS:]], axis=1).astype(BF16)
    gdnp = jnp.zeros((SUBLANES, LANES), F32)
    gdnp = gdnp.at[0, DN_HEADS:2 * DN_HEADS].set(dn_dt_bias[l])
    gdnp = gdnp.at[1, DN_HEADS:2 * DN_HEADS].set(dn_a_log[l])
    rwvec = jnp.stack([rw_w0[l], rw_a0[l], rw_k_k[l], rw_k_a[l], rw_r_k[l].reshape(RW_WIDTH),
                       rw_ln_w[l], rw_ln_b[l], jnp.zeros((RW_WIDTH,), F32)])
    zeros_lora = jnp.zeros((W_LORA, RW_WIDTH), F32)
    return {
        "w_in": w_in_p,
        "conv_w": jnp.concatenate([dn_conv_w[l], jnp.zeros((SUBLANES - CONV_W, DN_QKV), F32)], axis=0),
        "gdnp": gdnp,
        "dn_norm": dn_norm_w[l].reshape(1, LANES),
        "mu": rw_mu[l].reshape(1, RW_COLS),
        "rwvec": rwvec,
        "w_up": jnp.concatenate([rw_w_up[l], zeros_lora], axis=0),
        "a_up": jnp.concatenate([zeros_lora, rw_a_up[l]], axis=0),
        "g_up": rw_g_up[l],
    }


def _post_layer(x, mixed, ffn, l):
    return _post(x, mixed, ffn["w_out"][l], ffn["g_post_mix"][l], ffn["g_pre_ffn"][l],
                 ffn["w_gate"][l], ffn["w_up"][l], ffn["w_down"][l], ffn["g_post_ffn"][l])


def _trunk_carry(x, states, layers, ffn, *, batch, seq, rows, group):
    new_states = []
    for l in range(DEPTH):
        conv_s, dn_s, shift_s, rw_s = states[l]
        prep, conv_t, shift_t = _inproj_prep(x, ffn["g_pre_mix"][l], layers[l]["w_in"], conv_s, shift_s,
                                             layers[l], batch=batch, seq=seq, rows=rows)
        mixed, dn_new, rw_new = _mixer(prep, dn_s, rw_s, layers[l], batch=batch, seq=seq, rows=rows,
                                       group=group, chunk=group, n_pad=0, carry=True)
        x = _post_layer(x, mixed, ffn, l)
        new_states.append((conv_t, dn_new, shift_t, rw_new))
    return x, new_states


def _trunk_sample(x, state_dn_conv, state_dn, state_rw_shift, state_rw, layers, ffn, *, batch, seq_real):
    seq = SUBLANES
    n_pad = seq - seq_real
    new_states = []
    for l in range(DEPTH):
        proj = _inproj(x, ffn["g_pre_mix"][l], layers[l]["w_in"])
        proj3 = proj.reshape(batch, seq_real, P_COLS)
        hist = jnp.zeros((batch, n_pad, P_COLS), F32)
        hist = hist.at[:, n_pad - (CONV_W - 1):, C_QKV:C_QKV + DN_QKV].set(state_dn_conv[l])
        hist = hist.at[:, n_pad - 1, C_RW:].set(state_rw_shift[l])
        mix_in = jnp.concatenate([hist, proj3], axis=1).reshape(batch * seq, P_COLS)
        mixed, dn_new, rw_new = _mixer(mix_in, state_dn, state_rw, layers[l], batch=batch, seq=seq,
                                       rows=GROUP, group=GROUP, chunk=seq, n_pad=n_pad, carry=False,
                                       layer=l)
        mixed = mixed.reshape(batch, seq, D_MODEL)[:, n_pad:].reshape(batch * seq_real, D_MODEL)
        x = _post_layer(x, mixed, ffn, l)
        conv_new = proj3[:, seq_real - (CONV_W - 1):, C_QKV:C_QKV + DN_QKV]
        shift_new = proj3[:, seq_real - 1, C_RW:]
        new_states.append((conv_new, dn_new, shift_new, rw_new))
    return x, new_states


def kernel(x_prompt, x_sample, state_dn_conv, state_dn, state_rw_shift, state_rw, meta, g_pre_mix, g_post_mix, g_pre_ffn, g_post_ffn, w_in, dn_conv_w, dn_a_log, dn_dt_bias, dn_norm_w, rw_mu, rw_w0, rw_w_up, rw_a0, rw_a_up, rw_g_up, rw_k_k, rw_k_a, rw_r_k, rw_ln_w, rw_ln_b, w_out, w_gate, w_up, w_down):
    batch, seq, _ = x_prompt.shape
    dec_batch, dec_seq, _ = x_sample.shape
    layers = [_prep_layer_params(l, w_in, dn_conv_w, dn_a_log, dn_dt_bias, dn_norm_w, rw_mu, rw_w0,
                                 rw_w_up, rw_a0, rw_a_up, rw_g_up, rw_k_k, rw_k_a, rw_r_k, rw_ln_w,
                                 rw_ln_b) for l in range(DEPTH)]
    vec = lambda g: g.reshape(DEPTH, 1, D_MODEL)
    ffn = {"g_pre_mix": vec(g_pre_mix), "g_post_mix": vec(g_post_mix), "g_pre_ffn": vec(g_pre_ffn),
           "g_post_ffn": vec(g_post_ffn), "w_out": w_out.astype(BF16), "w_gate": w_gate.astype(BF16),
           "w_up": w_up.astype(BF16), "w_down": w_down.astype(BF16)}

    zero_states = [(jnp.zeros((1, SUBLANES, DN_QKV), F32), jnp.zeros((1, DN_HEADS, DN_HEAD_DIM, DN_HEAD_DIM), F32),
                    jnp.zeros((1, SUBLANES, RW_COLS), F32),
                    jnp.zeros((1, RW_HEADS, RW_HEAD_DIM, RW_HEAD_DIM), F32))] * DEPTH
    _, meta_states = _trunk_carry(meta.astype(F32), zero_states, layers, ffn,
                                  batch=1, seq=N_META, rows=N_META, group=N_META)

    yp, ps = _trunk_carry(x_prompt.reshape(batch * seq, D_MODEL), meta_states, layers, ffn,
                          batch=batch, seq=seq, rows=PROMPT_ROWS, group=GROUP)
    ps = [(c[:, SUBLANES - (CONV_W - 1):], d, s[:, SUBLANES - 1], r) for c, d, s, r in ps]

    ys, ss = _trunk_sample(x_sample.reshape(dec_batch * dec_seq, D_MODEL), state_dn_conv, state_dn,
                           state_rw_shift, state_rw, layers, ffn, batch=dec_batch, seq_real=dec_seq)

    def collect(sts):
        return tuple(jnp.stack([s[i] for s in sts]) for i in range(4))

    cp, dp, sp, rp = collect(ps)
    cs, ds, sh, rs = collect(ss)
    return (yp.reshape(batch, seq, D_MODEL), ys.reshape(dec_batch, dec_seq, D_MODEL),
            cp, dp, sp, rp, cs, ds, sh, rs)
```

```python
import functools
import math

import jax
import jax.numpy as jnp
from jax import lax
from jax.experimental import pallas as pl
from jax.experimental.pallas import tpu as pltpu

F32 = jnp.float32
BF16 = jnp.bfloat16

D_MODEL = 1024
DEPTH = 4
N_META = 16
DN_HEADS = 4
DN_HEAD_DIM = 128
DN_WIDTH = DN_HEADS * DN_HEAD_DIM
DN_QKV = 3 * DN_WIDTH
RW_HEADS = 8
RW_HEAD_DIM = 64
RW_WIDTH = RW_HEADS * RW_HEAD_DIM
W_LORA = 64
A_LORA = 64
G_LORA = 128
RW_COLS = 3 * RW_WIDTH + W_LORA + A_LORA + G_LORA
D_FF = 2816
CONV_W = 4
RMS_EPS = 1e-6
L2_EPS = 1e-6
GN_EPS = 64e-5

LANES = 128
SUBLANES = 8
VMEM_LIMIT = 56 * 1024 * 1024

C_QKV = 0
C_Z = DN_QKV
C_BG = C_Z + DN_WIDTH
C_RW = C_BG + LANES
P_COLS = C_RW + RW_COLS
GROUP = 64
PROMPT_ROWS = 512
SAMPLE_ROWS = 128


def _sigmoid(x):
    return 1.0 / (1.0 + jnp.exp(-x))


def _softplus(x):
    return jnp.maximum(x, 0.0) + jnp.log(1.0 + jnp.exp(-jnp.abs(x)))


def _rms(x, g):
    return x * lax.rsqrt(jnp.mean(x * x, axis=-1, keepdims=True) + RMS_EPS) * g


def _mm(a, b):
    return jnp.dot(a.astype(BF16), b.astype(BF16), preferred_element_type=F32)


def _mm_nt(a, b):
    return lax.dot_general(a.astype(BF16), b.astype(BF16), (((1,), (1,)), ((), ())),
                           preferred_element_type=F32)


def _mm_tn(a, b):
    return lax.dot_general(a.astype(BF16), b.astype(BF16), (((0,), (0,)), ((), ())),
                           preferred_element_type=F32)


def _mm_sel(sel, x):
    sel = sel.astype(BF16)
    x1 = x.astype(BF16)
    x2 = (x - x1.astype(F32)).astype(BF16)
    dot = lambda t: jnp.dot(sel, t, preferred_element_type=F32)
    return dot(x1) + dot(x2)


def _bdot(a, b):
    return jnp.dot(a.astype(BF16), b, preferred_element_type=F32)


def _inproj_body(x_ref, g_ref, w_ref, o_ref):
    h = _rms(x_ref[...], g_ref[...])
    o_ref[...] = _bdot(h, w_ref[...])


def _inproj(x, g, w):
    rows = x.shape[0]
    tm = min(rows, 256)
    return pl.pallas_call(
        _inproj_body,
        grid=(rows // tm,),
        in_specs=[
            pl.BlockSpec((tm, D_MODEL), lambda i: (i, 0)),
            pl.BlockSpec((1, D_MODEL), lambda i: (0, 0)),
            pl.BlockSpec((D_MODEL, P_COLS), lambda i: (0, 0)),
        ],
        out_specs=pl.BlockSpec((tm, P_COLS), lambda i: (i, 0)),
        out_shape=jax.ShapeDtypeStruct((rows, P_COLS), F32),
        compiler_params=pltpu.CompilerParams(
            dimension_semantics=("parallel",), vmem_limit_bytes=VMEM_LIMIT),
        name="inproj",
    )(x, g, w)


def _row_prep(proj, halo_qkv, halo_rw, cw, mu, rows):
    H = SUBLANES
    R = rows
    x_qkv = proj[:, C_QKV:C_QKV + DN_QKV]
    ext = jnp.concatenate([halo_qkv, x_qkv], axis=0)
    back = lambda a, k: pltpu.roll(a, k, 0)[H:H + R, :]
    conv = (back(ext, 3) * cw[0:1, :] + back(ext, 2) * cw[1:2, :] + back(ext, 1) * cw[2:3, :]
            + x_qkv * cw[3:4, :])
    qkv = conv * _sigmoid(conv)

    def l2n(x):
        return x * lax.rsqrt(jnp.sum(x * x, axis=-1, keepdims=True) + L2_EPS)

    hl = lambda base, h: slice(base + h * LANES, base + (h + 1) * LANES)
    parts = ([l2n(qkv[:, hl(0, h)]) * (DN_HEAD_DIM ** -0.5) for h in range(DN_HEADS)]
             + [l2n(qkv[:, hl(DN_WIDTH, h)]) for h in range(DN_HEADS)]
             + [qkv[:, hl(2 * DN_WIDTH, h)] for h in range(DN_HEADS)])
    cols = proj[:, C_RW:C_RW + RW_COLS]
    prev = back(jnp.concatenate([halo_rw, cols], axis=0), 1)
    xs = cols + (prev - cols) * mu
    return parts, xs, x_qkv[R - H:R, :], cols[R - H:R, :]


def _inproj_prep_body(x_ref, g_ref, w_ref, conv0_ref, shift0_ref, cw_ref, mu_ref,
                      o_ref, ctail_ref, stail_ref, halo_qkv, halo_rw, *, rows):
    @pl.when(pl.program_id(1) == 0)
    def _():
        halo_qkv[...] = conv0_ref[0]
        halo_rw[...] = shift0_ref[0]

    proj = _bdot(_rms(x_ref[...], g_ref[...]), w_ref[...])
    parts, xs, tail_qkv, tail_rw = _row_prep(proj, halo_qkv[...], halo_rw[...], cw_ref[...],
                                             mu_ref[...], rows)
    for idx, part in enumerate(parts):
        o_ref[:, idx * LANES:(idx + 1) * LANES] = part
    o_ref[:, C_Z:C_RW] = proj[:, C_Z:C_RW]
    o_ref[:, C_RW:C_RW + RW_COLS] = xs
    halo_qkv[...] = tail_qkv
    halo_rw[...] = tail_rw
    ctail_ref[0] = tail_qkv
    stail_ref[0] = tail_rw


def _inproj_prep(x, g, w, conv0, shift0, lp, *, batch, seq, rows):
    nt = seq // rows
    bsel = (lambda b: b) if conv0.shape[0] == batch else (lambda b: 0)
    full = lambda shape: pl.BlockSpec(shape, lambda b, t: (0,) * len(shape))
    return pl.pallas_call(
        functools.partial(_inproj_prep_body, rows=rows),
        grid=(batch, nt),
        in_specs=[
            pl.BlockSpec((rows, D_MODEL), lambda b, t: (b * nt + t, 0)),
            full((1, D_MODEL)), full((D_MODEL, P_COLS)),
            pl.BlockSpec((1, SUBLANES, DN_QKV), lambda b, t: (bsel(b), 0, 0)),
            pl.BlockSpec((1, SUBLANES, RW_COLS), lambda b, t: (bsel(b), 0, 0)),
            full((SUBLANES, DN_QKV)), full((1, RW_COLS)),
        ],
        out_specs=[
            pl.BlockSpec((rows, P_COLS), lambda b, t: (b * nt + t, 0)),
            pl.BlockSpec((1, SUBLANES, DN_QKV), lambda b, t: (b, 0, 0)),
            pl.BlockSpec((1, SUBLANES, RW_COLS), lambda b, t: (b, 0, 0)),
        ],
        out_shape=[jax.ShapeDtypeStruct((batch * seq, P_COLS), F32),
                   jax.ShapeDtypeStruct((batch, SUBLANES, DN_QKV), F32),
                   jax.ShapeDtypeStruct((batch, SUBLANES, RW_COLS), F32)],
        scratch_shapes=[pltpu.VMEM((SUBLANES, DN_QKV), F32), pltpu.VMEM((SUBLANES, RW_COLS), F32)],
        compiler_params=pltpu.CompilerParams(
            dimension_semantics=("parallel", "arbitrary"), vmem_limit_bytes=VMEM_LIMIT),
        name="inproj_prep",
    )(x, g, w, conv0, shift0, lp["conv_w"], lp["mu"])


FF_CHUNK = D_FF // 2


def _post_body(x_ref, m_ref, wo_ref, g1_ref, g2_ref, wg_ref, wu_ref, wd_ref, g3_ref, o_ref):
    mixed = _bdot(m_ref[...], wo_ref[...])
    x1 = x_ref[...] + _rms(mixed, g1_ref[...])
    h = _rms(x1, g2_ref[...]).astype(BF16)
    acc = jnp.zeros(x1.shape, F32)
    for c in range(D_FF // FF_CHUNK):
        sl = slice(c * FF_CHUNK, (c + 1) * FF_CHUNK)
        gate = jnp.dot(h, wg_ref[:, sl], preferred_element_type=F32)
        up = jnp.dot(h, wu_ref[:, sl], preferred_element_type=F32)
        act = gate * _sigmoid(gate) * up
        acc = acc + _bdot(act, wd_ref[sl, :])
    o_ref[...] = x1 + _rms(acc, g3_ref[...])


def _post(x, mixed, wo, g1, g2, wg, wu, wd, g3):
    rows = x.shape[0]
    tm = min(rows, 512)
    row_spec = pl.BlockSpec((tm, D_MODEL), lambda i: (i, 0))
    vec_spec = pl.BlockSpec((1, D_MODEL), lambda i: (0, 0))

    def resident(shape):
        return pl.BlockSpec(shape, lambda i: (0, 0), pipeline_mode=pl.Buffered(1))

    return pl.pallas_call(
        _post_body,
        grid=(rows // tm,),
        in_specs=[row_spec, row_spec, resident((D_MODEL, D_MODEL)), vec_spec, vec_spec,
                  resident((D_MODEL, D_FF)), resident((D_MODEL, D_FF)), resident((D_FF, D_MODEL)),
                  vec_spec],
        out_specs=row_spec,
        out_shape=jax.ShapeDtypeStruct((rows, D_MODEL), F32),
        compiler_params=pltpu.CompilerParams(
            dimension_semantics=("parallel",), vmem_limit_bytes=VMEM_LIMIT),
        name="post",
    )(x, mixed, wo, g1, g2, wg, wu, wd, g3)


def _inv_unit_lower_many(mats, eye, chunk):
    ns = [-a for a in mats]
    ps = [eye + n for n in ns]
    for _ in range(max(int(math.ceil(math.log2(chunk))) - 1, 0)):
        ns = [_mm(n, n) for n in ns]
        ps = [p + _mm(p, n) for p, n in zip(ps, ns)]
    return ps


def _stack(*parts):
    return jnp.concatenate(parts, axis=0)


def _pair_blockdiag(s0, s1):
    zero = jnp.zeros_like(s0)
    return jnp.concatenate([jnp.concatenate([s0, zero], axis=1),
                            jnp.concatenate([zero, s1], axis=1)], axis=0)


def _mixer_body(proj_ref, dn0_ref, rw0_ref, convw_ref, gdnp_ref, dnnorm_ref,
                mu_ref, rwvec_ref, wup_ref, aup_ref, gup_ref,
                mixed_ref, dn_out_ref, rw_out_ref, *scratch, rows, group, chunk, n_pad, carry):
    R, GS, C = rows, group, chunk
    NG = R // GS
    NS = GS // C
    S2 = 2 * GS
    H = SUBLANES
    shift = int(math.log2(C))
    t = pl.program_id(1)

    D = RW_HEAD_DIM
    if carry:
        dn_s, rw_s = scratch

        @pl.when(t == 0)
        def _():
            dn_s[...] = dn0_ref[0]
            for p in range(RW_HEADS // 2):
                rw_s[p] = _pair_blockdiag(rw0_ref[0, 2 * p], rw0_ref[0, 2 * p + 1])

    ri = lax.broadcasted_iota(jnp.int32, (S2, S2), 0)
    ci = lax.broadcasted_iota(jnp.int32, (S2, S2), 1)
    same = (ri >> shift) == (ci >> shift)
    strict = same & (ci < ri)
    incl = same & (ci <= ri)
    eye2 = (ri == ci).astype(F32)
    r1 = lax.broadcasted_iota(jnp.int32, (GS, GS), 0)
    c1 = lax.broadcasted_iota(jnp.int32, (GS, GS), 1)
    lt_blk = (((r1 >> shift) == (c1 >> shift)) & (c1 <= r1)).astype(F32)
    last_sel = (c1 == (((r1 >> shift) << shift) + (C - 1))).astype(F32)
    lane = lax.broadcasted_iota(jnp.int32, (1, LANES), 1)
    m0 = (lane < RW_HEAD_DIM).astype(F32)
    m1 = 1.0 - m0
    bi = lax.broadcasted_iota(jnp.int32, (LANES, LANES), 0)
    bj = lax.broadcasted_iota(jnp.int32, (LANES, LANES), 1)
    bd = ((bi >= RW_HEAD_DIM) == (bj >= RW_HEAD_DIM)).astype(F32)
    if n_pad:
        rowm = ((lax.broadcasted_iota(jnp.int32, (R, 1), 0) & (C - 1)) >= n_pad).astype(F32)
    else:
        rowm = None
    grp = [slice(g * GS, (g + 1) * GS) for g in range(NG)]

    def block_last(x):
        if NS == 1:
            return jnp.broadcast_to(x[GS - 1:GS, :], x.shape)
        return _mm_sel(last_sel, x)

    hl = lambda base, h: slice(base + h * LANES, base + (h + 1) * LANES)
    if carry:
        qn = [proj_ref[:, hl(0, h)] for h in range(DN_HEADS)]
        kn = [proj_ref[:, hl(DN_WIDTH, h)] for h in range(DN_HEADS)]
        vv = [proj_ref[:, hl(2 * DN_WIDTH, h)] for h in range(DN_HEADS)]
        xs = proj_ref[:, C_RW:C_RW + RW_COLS]
    else:
        parts, xs, _, _ = _row_prep(proj_ref, jnp.zeros((H, DN_QKV), F32), jnp.zeros((H, RW_COLS), F32),
                                    convw_ref[...], mu_ref[...], R)
        qn, kn, vv = parts[:DN_HEADS], parts[DN_HEADS:2 * DN_HEADS], parts[2 * DN_HEADS:]

    z = proj_ref[:, C_Z:C_Z + DN_WIDTH]
    bg = proj_ref[:, C_BG:C_BG + LANES]
    gp = gdnp_ref[...]
    beta_all = _sigmoid(bg)
    glog_all = -jnp.exp(gp[1:2, :]) * _softplus(bg + gp[0:1, :])
    if rowm is not None:
        beta_all = beta_all * rowm
        glog_all = glog_all * rowm
    gcum = [_mm_sel(lt_blk, glog_all[rs]) for rs in grp]
    glast = [block_last(gc) for gc in gcum]

    r = xs[:, 0:RW_WIDTH]
    kr = xs[:, RW_WIDTH:2 * RW_WIDTH]
    vr = xs[:, 2 * RW_WIDTH:3 * RW_WIDTH]
    wa = xs[:, 3 * RW_WIDTH:3 * RW_WIDTH + LANES]
    gd = xs[:, 3 * RW_WIDTH + LANES:RW_COLS]
    rv = rwvec_ref[...]
    w0, a0, k_k, k_a, r_k, ln_w, ln_b = (rv[i:i + 1, :] for i in range(7))
    w_log = -_softplus(-(w0 + _mm(jnp.tanh(wa), wup_ref[...]))) - 0.5
    lw = -jnp.exp(w_log)
    a = _sigmoid(a0 + _mm(wa, aup_ref[...]))
    g = _mm(_sigmoid(gd), gup_ref[...])
    kk_raw = kr * k_k
    kr2 = kr * (1.0 + (a - 1.0) * k_a)
    bonus_in = r * kr2 * r_k
    if rowm is not None:
        lw = lw * rowm
        kr2 = kr2 * rowm
    cl_g = [_mm_sel(lt_blk, lw[rs]) for rs in grp]
    cl = cl_g[0] if NG == 1 else jnp.concatenate(cl_g, axis=0)
    p_last = [jnp.exp(block_last(c)) for c in cl_g]
    e_cl = jnp.exp(cl)
    e_prev = jnp.exp(cl - lw)
    e_inv = jnp.exp(-cl)
    r_t = r * e_cl
    k_t = kr2 * e_inv
    pl_ = lambda p: slice(p * LANES, (p + 1) * LANES)
    kk = []
    for p in range(RW_HEADS // 2):
        kp = kk_raw[:, pl_(p)]
        kp = kp * lax.rsqrt(_mm(kp * kp, bd) + L2_EPS)
        if rowm is not None:
            kp = kp * rowm
        kk.append(kp)

    gd_keys = [(gi, p) for gi in range(NG) for p in range(DN_HEADS // 2)]

    def gstack(arrs, gi, p):
        return _stack(arrs[2 * p][grp[gi]], arrs[2 * p + 1][grp[gi]])

    def gcol(mat, col, p):
        return _stack(mat[:, col + 2 * p:col + 2 * p + 1], mat[:, col + 2 * p + 1:col + 2 * p + 2])

    q2 = [gstack(qn, gi, p) for gi, p in gd_keys]
    k2 = [gstack(kn, gi, p) for gi, p in gd_keys]
    v2 = [gstack(vv, gi, p) for gi, p in gd_keys]
    beta2 = [gcol(beta_all[grp[gi]], 0, p) for gi, p in gd_keys]
    gc2 = [gcol(gcum[gi], DN_HEADS, p) for gi, p in gd_keys]
    gl2 = [gcol(glast[gi], DN_HEADS, p) for gi, p in gd_keys]
    gct = [jnp.transpose(gc) for gc in gcum]
    rb = [jnp.concatenate([gct[gi][DN_HEADS + 2 * p:DN_HEADS + 2 * p + 1, :],
                           gct[gi][DN_HEADS + 2 * p + 1:DN_HEADS + 2 * p + 2, :]], axis=1)
          for gi, p in gd_keys]
    dec = [jnp.exp(jnp.where(incl, c - b, 0.0)) for c, b in zip(gc2, rb)]
    kb = [k * b for k, b in zip(k2, beta2)]
    s_kk = [_mm_nt(x, k) for x, k in zip(kb, k2)]
    s_qk = [_mm_nt(x, k) for x, k in zip(q2, k2)]
    a_mat = [jnp.where(strict, s * d, 0.0) for s, d in zip(s_kk, dec)]
    qk = [jnp.where(incl, s * d, 0.0) for s, d in zip(s_qk, dec)]
    t_dn = _inv_unit_lower_many(a_mat, eye2, C)
    eg = [jnp.exp(c) for c in gc2]
    uw = [_mm(tm, jnp.concatenate([v * b, x * e], axis=1))
          for tm, v, b, x, e in zip(t_dn, v2, beta2, kb, eg)]
    qo = [_mm(m, x) for m, x in zip(qk, uw)]
    u_dn = [x[:, :LANES] for x in uw]
    w_dn = [x[:, LANES:] for x in uw]
    o2_dn = [x[:, :LANES] for x in qo]
    qg2 = [q * e - x[:, LANES:] for q, e, x in zip(q2, eg, qo)]
    k_dec = [k * jnp.exp(l - c) for k, l, c in zip(k2, gl2, gc2)]
    e_gl = [jnp.exp(l) for l in gl2]

    rw_keys = [(gi, p) for gi in range(NG) for p in range(RW_HEADS // 2)]
    kk_t = [kk[p][grp[gi]] * e_prev[grp[gi], pl_(p)] for gi, p in rw_keys]
    a_t = [kk[p][grp[gi]] * a[grp[gi], pl_(p)] * e_inv[grp[gi], pl_(p)] for gi, p in rw_keys]
    rt_p = [r_t[grp[gi], pl_(p)] for gi, p in rw_keys]
    kt_p = [k_t[grp[gi], pl_(p)] for gi, p in rw_keys]
    v_p = [vr[grp[gi], pl_(p)] for gi, p in rw_keys]
    l_kk = [_stack(x * m0, x * m1) for x in kk_t]
    l_r = [_stack(x * m0, x * m1) for x in rt_p]
    vs = [_stack(x * m0, x * m1) for x in v_p]
    if S2 % LANES == 0:
        sc = [_mm_nt(_stack(lk, lr), _stack(x, x, y, y))
              for lk, lr, x, y in zip(l_kk, l_r, a_t, kt_p)]
        m_ka = [jnp.where(strict, s[:S2, :S2], 0.0) for s in sc]
        m_kv = [jnp.where(strict, s[:S2, S2:], 0.0) for s in sc]
        a_ra = [jnp.where(incl, s[S2:, :S2], 0.0) for s in sc]
        a_rk = [jnp.where(incl, s[S2:, S2:], 0.0) for s in sc]
    else:
        ra2 = [_stack(x, x) for x in a_t]
        rk2 = [_stack(x, x) for x in kt_p]
        m_ka = [jnp.where(strict, _mm_nt(x, y), 0.0) for x, y in zip(l_kk, ra2)]
        m_kv = [jnp.where(strict, _mm_nt(x, y), 0.0) for x, y in zip(l_kk, rk2)]
        a_ra = [jnp.where(incl, _mm_nt(x, y), 0.0) for x, y in zip(l_r, ra2)]
        a_rk = [jnp.where(incl, _mm_nt(x, y), 0.0) for x, y in zip(l_r, rk2)]
    t_rw = _inv_unit_lower_many(m_ka, eye2, C)
    mv = [_mm(m, x) for m, x in zip(m_kv, vs)]
    tlw = [_mm(tm, jnp.concatenate([lk, x], axis=1)) for tm, lk, x in zip(t_rw, l_kk, mv)]
    aa = [_mm(m, x) for m, x in zip(a_ra, tlw)]
    av = [_mm(m, x) for m, x in zip(a_rk, vs)]
    tl = [x[:, :LANES] for x in tlw]
    w2 = [x[:, LANES:] for x in tlw]
    ql = [lr - x[:, :LANES] for lr, x in zip(l_r, aa)]
    y2 = [x - y[:, LANES:] for x, y in zip(av, aa)]

    o_rows = [[None] * NG for _ in range(DN_HEADS)]
    y_rows = [[None] * NG for _ in range(RW_HEADS // 2)]
    if carry:
        s_dn = [dn_s[h] for h in range(DN_HEADS)]
        s_rw = [rw_s[p] for p in range(RW_HEADS // 2)]
        for gi in range(NG):
            hsl = lambda e: slice(e * GS, (e + 1) * GS)
            dn_i = [(gd_keys.index((gi, h // 2)), h % 2) for h in range(DN_HEADS)]
            rw_i = [rw_keys.index((gi, p)) for p in range(RW_HEADS // 2)]
            ss_dn = [_mm(_stack(w_dn[i][hsl(e)], qg2[i][hsl(e)]), s_dn[h])
                     for h, (i, e) in enumerate(dn_i)]
            ss_rw = [_mm_nt(_stack(tl[i], ql[i]), s_rw[p]) for p, i in enumerate(rw_i)]
            v_new = [u_dn[i][hsl(e)] - ss[:GS] for ss, (i, e) in zip(ss_dn, dn_i)]
            for h, (ss, (i, e)) in enumerate(zip(ss_dn, dn_i)):
                o_rows[h][gi] = ss[GS:] + o2_dn[i][hsl(e)]
            us = [ss[:S2] + w2[i] for ss, i in zip(ss_rw, rw_i)]
            for p, (ss, i) in enumerate(zip(ss_rw, rw_i)):
                ys = ss[S2:] + y2[i]
                y_rows[p][gi] = ys[:GS] + ys[GS:]
            u1 = [x[:GS] + x[GS:] for x in us]
            s_dn = [s * e_gl[i][e * GS:e * GS + 1, :] + _mm_tn(k_dec[i][hsl(e)], vn)
                    for s, vn, (i, e) in zip(s_dn, v_new, dn_i)]
            s_rw = [(s + _mm_tn(_stack(v_p[i], -x), _stack(kt_p[i], a_t[i]))) * bd
                    * p_last[gi][0:1, pl_(p)]
                    for p, (s, x, i) in enumerate(zip(s_rw, u1, rw_i))]
        for h in range(DN_HEADS):
            dn_s[h] = s_dn[h]
        for p in range(RW_HEADS // 2):
            rw_s[p] = s_rw[p]

        @pl.when(t == pl.num_programs(1) - 1)
        def _():
            dn_out_ref[0] = dn_s[...]
            for p in range(RW_HEADS // 2):
                rw_out_ref[0, 2 * p] = rw_s[p][:D, :D]
                rw_out_ref[0, 2 * p + 1] = rw_s[p][D:, D:]
    else:
        for gi in range(NG):
            seqs = range(NS)
            rsl = lambda e, j: slice(e * GS + j * C, e * GS + (j + 1) * C)
            dn_items = [(j, h, gd_keys.index((gi, h // 2)), h % 2) for j in seqs for h in range(DN_HEADS)]
            rw_items = [(j, p, rw_keys.index((gi, p))) for j in seqs for p in range(RW_HEADS // 2)]
            s_dn = [dn0_ref[gi * NS + j, h] for j, h, _, _ in dn_items]
            s_rw = [_pair_blockdiag(rw0_ref[gi * NS + j, 2 * p], rw0_ref[gi * NS + j, 2 * p + 1])
                    for j, p, _ in rw_items]
            ss_dn = [_mm(_stack(w_dn[i][rsl(e, j)], qg2[i][rsl(e, j)]), s)
                     for s, (j, h, i, e) in zip(s_dn, dn_items)]
            ss_rw = [_mm_nt(_stack(tl[i][rsl(0, j)], tl[i][rsl(1, j)], ql[i][rsl(0, j)], ql[i][rsl(1, j)]), s)
                     for s, (j, p, i) in zip(s_rw, rw_items)]
            v_new = [u_dn[i][rsl(e, j)] - ss[:C] for ss, (j, h, i, e) in zip(ss_dn, dn_items)]
            o_seq = [ss[C:] + o2_dn[i][rsl(e, j)] for ss, (j, h, i, e) in zip(ss_dn, dn_items)]
            us = [ss[:2 * C] + _stack(w2[i][rsl(0, j)], w2[i][rsl(1, j)]) for ss, (j, p, i) in zip(ss_rw, rw_items)]
            ysq = [ss[2 * C:] + _stack(y2[i][rsl(0, j)], y2[i][rsl(1, j)]) for ss, (j, p, i) in zip(ss_rw, rw_items)]
            u1 = [x[:C] + x[C:] for x in us]
            y_seq = [x[:C] + x[C:] for x in ysq]
            for s, vn, (j, h, i, e) in zip(s_dn, v_new, dn_items):
                row = e * GS + j * C
                dn_out_ref[gi * NS + j, h] = (s * e_gl[i][row:row + 1, :]
                                             + _mm_tn(k_dec[i][rsl(e, j)], vn))
            for s, x, (j, p, i) in zip(s_rw, u1, rw_items):
                sq = slice(j * C, (j + 1) * C)
                s_new = ((s + _mm_tn(_stack(v_p[i][sq], -x), _stack(kt_p[i][sq], a_t[i][sq])))
                         * bd * p_last[gi][j * C:j * C + 1, pl_(p)])
                rw_out_ref[gi * NS + j, 2 * p] = s_new[:D, :D]
                rw_out_ref[gi * NS + j, 2 * p + 1] = s_new[D:, D:]
            for h in range(DN_HEADS):
                parts = [o for o, (j, hh, _, _) in zip(o_seq, dn_items) if hh == h]
                o_rows[h][gi] = parts[0] if NS == 1 else jnp.concatenate(parts, axis=0)
            for p in range(RW_HEADS // 2):
                parts = [y for y, (j, pp, _) in zip(y_seq, rw_items) if pp == p]
                y_rows[p][gi] = parts[0] if NS == 1 else jnp.concatenate(parts, axis=0)

    cat = lambda parts: parts[0] if len(parts) == 1 else jnp.concatenate(parts, axis=0)
    dn_w = dnnorm_ref[...]
    for h in range(DN_HEADS):
        zh = z[:, h * LANES:(h + 1) * LANES]
        mixed_ref[:, h * LANES:(h + 1) * LANES] = _rms(cat(o_rows[h]), dn_w) * (zh * _sigmoid(zh))
    for p in range(RW_HEADS // 2):
        sl = pl_(p)
        y = cat(y_rows[p])
        mu_g = _mm(y, bd) * (1.0 / RW_HEAD_DIM)
        yc = y - mu_g
        var = _mm(yc * yc, bd) * (1.0 / RW_HEAD_DIM)
        yn = yc * lax.rsqrt(var + GN_EPS) * ln_w[:, sl] + ln_b[:, sl]
        bonus = _mm(bonus_in[:, sl], bd) * vr[:, sl]
        mixed_ref[:, DN_WIDTH + p * LANES:DN_WIDTH + (p + 1) * LANES] = (yn + bonus) * g[:, sl]


def _mixer(proj, dn0, rw0, lp, *, batch, seq, rows, group, chunk, n_pad, carry, layer=None):
    full = lambda shape: pl.BlockSpec(shape, lambda b, t: (0,) * len(shape))
    dn_blk = (DN_HEADS, DN_HEAD_DIM, DN_HEAD_DIM)
    rw_blk = (RW_HEADS, RW_HEAD_DIM, RW_HEAD_DIM)
    if carry:
        nt = seq // rows
        grid = (batch, nt)
        bsel = (lambda b: b) if dn0.shape[0] == batch else (lambda b: 0)
        row_map = lambda b, t: (b * nt + t, 0)
        st_in = [pl.BlockSpec((1,) + blk, lambda b, t: (bsel(b), 0, 0, 0)) for blk in (dn_blk, rw_blk)]
        st_out = [pl.BlockSpec((1,) + blk, lambda b, t: (b, 0, 0, 0)) for blk in (dn_blk, rw_blk)]
        scratch = [pltpu.VMEM((DN_HEADS, LANES, LANES), F32), pltpu.VMEM((RW_HEADS // 2, LANES, LANES), F32)]
    else:
        nseq = rows // chunk
        grid = (batch // nseq, 1)
        row_map = lambda b, t: (b, 0)
        st_in = [pl.BlockSpec((None, nseq) + blk, lambda b, t: (layer, b, 0, 0, 0)) for blk in (dn_blk, rw_blk)]
        st_out = [pl.BlockSpec((nseq,) + blk, lambda b, t: (b, 0, 0, 0)) for blk in (dn_blk, rw_blk)]
        scratch = []
    return pl.pallas_call(
        functools.partial(_mixer_body, rows=rows, group=group, chunk=chunk, n_pad=n_pad, carry=carry),
        grid=grid,
        in_specs=[
            pl.BlockSpec((rows, P_COLS), row_map),
            *st_in,
            full((SUBLANES, DN_QKV)), full((SUBLANES, LANES)), full((1, LANES)),
            full((1, RW_COLS)), full((SUBLANES, RW_WIDTH)),
            full((LANES, RW_WIDTH)), full((LANES, RW_WIDTH)), full((G_LORA, RW_WIDTH)),
        ],
        out_specs=[pl.BlockSpec((rows, D_MODEL), row_map), *st_out],
        out_shape=[jax.ShapeDtypeStruct((batch * seq, D_MODEL), F32),
                   jax.ShapeDtypeStruct((batch,) + dn_blk, F32),
                   jax.ShapeDtypeStruct((batch,) + rw_blk, F32)],
        scratch_shapes=scratch,
        compiler_params=pltpu.CompilerParams(
            dimension_semantics=("parallel", "arbitrary"), vmem_limit_bytes=VMEM_LIMIT),
        name="mixer",
    )(proj, dn0, rw0, lp["conv_w"], lp["gdnp"], lp["dn_norm"], lp["mu"],
      lp["rwvec"], lp["w_up"], lp["a_up"], lp["g_up"])


def _prep_layer_params(l, w_in, dn_conv_w, dn_a_log, dn_dt_bias, dn_norm_w, rw_mu, rw_w0, rw_w_up,
                       rw_a0, rw_a_up, rw_g_up, rw_k_k, rw_k_a, rw_r_k, rw_ln_w, rw_ln_b):
    off = DN_QKV + DN_WIDTH
    wi = w_in[l]
    w_in_p = jnp.concatenate(
        [wi[:, :off + 2 * DN_HEADS], jnp.zeros((D_MODEL, LANES - 2 * DN_HEADS), F32),
         wi[:, off + 2 * DN_HEADS:]], axis=1).astype(BF16)
    gdnp = jnp.zeros((SUBLANES, LANES), F32)
    gdnp = gdnp.at[0, DN_HEADS:2 * DN_HEADS].set(dn_dt_bias[l])
    gdnp = gdnp.at[1, DN_HEADS:2 * DN_HEADS].set(dn_a_log[l])
    rwvec = jnp.stack([rw_w0[l], rw_a0[l], rw_k_k[l], rw_k_a[l], rw_r_k[l].reshape(RW_WIDTH),
                       rw_ln_w[l], rw_ln_b[l], jnp.zeros((RW_WIDTH,), F32)])
    zeros_lora = jnp.zeros((W_LORA, RW_WIDTH), F32)
    return {
        "w_in": w_in_p,
        "conv_w": jnp.concatenate([dn_conv_w[l], jnp.zeros((SUBLANES - CONV_W, DN_QKV), F32)], axis=0),
        "gdnp": gdnp,
        "dn_norm": dn_norm_w[l].reshape(1, LANES),
        "mu": rw_mu[l].reshape(1, RW_COLS),
        "rwvec": rwvec,
        "w_up": jnp.concatenate([rw_w_up[l], zeros_lora], axis=0),
        "a_up": jnp.concatenate([zeros_lora, rw_a_up[l]], axis=0),
        "g_up": rw_g_up[l],
    }


def _post_layer(x, mixed, ffn, l):
    return _post(x, mixed, ffn["w_out"][l], ffn["g_post_mix"][l], ffn["g_pre_ffn"][l],
                 ffn["w_gate"][l], ffn["w_up"][l], ffn["w_down"][l], ffn["g_post_ffn"][l])


def _trunk_carry(x, states, layers, ffn, *, batch, seq, rows, group):
    new_states = []
    for l in range(DEPTH):
        conv_s, dn_s, shift_s, rw_s = states[l]
        prep, conv_t, shift_t = _inproj_prep(x, ffn["g_pre_mix"][l], layers[l]["w_in"], conv_s, shift_s,
                                             layers[l], batch=batch, seq=seq, rows=rows)
        mixed, dn_new, rw_new = _mixer(prep, dn_s, rw_s, layers[l], batch=batch, seq=seq, rows=rows,
                                       group=group, chunk=group, n_pad=0, carry=True)
        x = _post_layer(x, mixed, ffn, l)
        new_states.append((conv_t, dn_new, shift_t, rw_new))
    return x, new_states


def _trunk_sample(x, state_dn_conv, state_dn, state_rw_shift, state_rw, layers, ffn, *, batch, seq_real):
    seq = SUBLANES
    n_pad = seq - seq_real
    new_states = []
    for l in range(DEPTH):
        proj = _inproj(x, ffn["g_pre_mix"][l], layers[l]["w_in"])
        proj3 = proj.reshape(batch, seq_real, P_COLS)
        hist = jnp.zeros((batch, n_pad, P_COLS), F32)
        hist = hist.at[:, n_pad - (CONV_W - 1):, C_QKV:C_QKV + DN_QKV].set(state_dn_conv[l])
        hist = hist.at[:, n_pad - 1, C_RW:].set(state_rw_shift[l])
        mix_in = jnp.concatenate([hist, proj3], axis=1).reshape(batch * seq, P_COLS)
        mixed, dn_new, rw_new = _mixer(mix_in, state_dn, state_rw, layers[l], batch=batch, seq=seq,
                                       rows=SAMPLE_ROWS, group=GROUP, chunk=seq, n_pad=n_pad, carry=False,
                                       layer=l)
        mixed = mixed.reshape(batch, seq, D_MODEL)[:, n_pad:].reshape(batch * seq_real, D_MODEL)
        x = _post_layer(x, mixed, ffn, l)
        conv_new = proj3[:, seq_real - (CONV_W - 1):, C_QKV:C_QKV + DN_QKV]
        shift_new = proj3[:, seq_real - 1, C_RW:]
        new_states.append((conv_new, dn_new, shift_new, rw_new))
    return x, new_states


def kernel(x_prompt, x_sample, state_dn_conv, state_dn, state_rw_shift, state_rw, meta, g_pre_mix, g_post_mix, g_pre_ffn, g_post_ffn, w_in, dn_conv_w, dn_a_log, dn_dt_bias, dn_norm_w, rw_mu, rw_w0, rw_w_up, rw_a0, rw_a_up, rw_g_up, rw_k_k, rw_k_a, rw_r_k, rw_ln_w, rw_ln_b, w_out, w_gate, w_up, w_down):
    batch, seq, _ = x_prompt.shape
    dec_batch, dec_seq, _ = x_sample.shape
    layers = [_prep_layer_params(l, w_in, dn_conv_w, dn_a_log, dn_dt_bias, dn_norm_w, rw_mu, rw_w0,
                                 rw_w_up, rw_a0, rw_a_up, rw_g_up, rw_k_k, rw_k_a, rw_r_k, rw_ln_w,
                                 rw_ln_b) for l in range(DEPTH)]
    vec = lambda g: g.reshape(DEPTH, 1, D_MODEL)
    ffn = {"g_pre_mix": vec(g_pre_mix), "g_post_mix": vec(g_post_mix), "g_pre_ffn": vec(g_pre_ffn),
           "g_post_ffn": vec(g_post_ffn), "w_out": w_out.astype(BF16), "w_gate": w_gate.astype(BF16),
           "w_up": w_up.astype(BF16), "w_down": w_down.astype(BF16)}

    zero_states = [(jnp.zeros((1, SUBLANES, DN_QKV), F32), jnp.zeros((1, DN_HEADS, DN_HEAD_DIM, DN_HEAD_DIM), F32),
                    jnp.zeros((1, SUBLANES, RW_COLS), F32),
                    jnp.zeros((1, RW_HEADS, RW_HEAD_DIM, RW_HEAD_DIM), F32))] * DEPTH
    _, meta_states = _trunk_carry(meta.astype(F32), zero_states, layers, ffn,
                                  batch=1, seq=N_META, rows=N_META, group=N_META)

    yp, ps = _trunk_carry(x_prompt.reshape(batch * seq, D_MODEL), meta_states, layers, ffn,
                          batch=batch, seq=seq, rows=PROMPT_ROWS, group=GROUP)
    ps = [(c[:, SUBLANES - (CONV_W - 1):], d, s[:, SUBLANES - 1], r) for c, d, s, r in ps]

    ys, ss = _trunk_sample(x_sample.reshape(dec_batch * dec_seq, D_MODEL), state_dn_conv, state_dn,
                           state_rw_shift, state_rw, layers, ffn, batch=dec_batch, seq_real=dec_seq)

    def collect(sts):
        return tuple(jnp.stack([s[i] for s in sts]) for i in range(4))

    cp, dp, sp, rp = collect(ps)
    cs, ds, sh, rs = collect(ss)
    return (yp.reshape(batch, seq, D_MODEL), ys.reshape(dec_batch, dec_seq, D_MODEL),
            cp, dp, sp, rp, cs, ds, sh, rs)
```

```python
import functools
import math

import jax
import jax.numpy as jnp
from jax import lax
from jax.experimental import pallas as pl
from jax.experimental.pallas import tpu as pltpu

F32 = jnp.float32
BF16 = jnp.bfloat16

D_MODEL = 1024
DEPTH = 4
N_META = 16
DN_HEADS = 4
DN_HEAD_DIM = 128
DN_WIDTH = DN_HEADS * DN_HEAD_DIM
DN_QKV = 3 * DN_WIDTH
RW_HEADS = 8
RW_HEAD_DIM = 64
RW_WIDTH = RW_HEADS * RW_HEAD_DIM
W_LORA = 64
A_LORA = 64
G_LORA = 128
RW_COLS = 3 * RW_WIDTH + W_LORA + A_LORA + G_LORA
D_FF = 2816
CONV_W = 4
RMS_EPS = 1e-6
L2_EPS = 1e-6
GN_EPS = 64e-5

LANES = 128
SUBLANES = 8
VMEM_LIMIT = 56 * 1024 * 1024

C_QKV = 0
C_Z = DN_QKV
C_BG = C_Z + DN_WIDTH
C_RW = C_BG + LANES
P_COLS = C_RW + RW_COLS
GROUP = 64
PROMPT_ROWS = 512
SAMPLE_ROWS = 128


def _sigmoid(x):
    return 1.0 / (1.0 + jnp.exp(-x))


def _softplus(x):
    return jnp.maximum(x, 0.0) + jnp.log(1.0 + jnp.exp(-jnp.abs(x)))


def _rms(x, g):
    return x * lax.rsqrt(jnp.mean(x * x, axis=-1, keepdims=True) + RMS_EPS) * g


def _mm(a, b):
    return jnp.dot(a.astype(BF16), b.astype(BF16), preferred_element_type=F32)


def _mm_nt(a, b):
    return lax.dot_general(a.astype(BF16), b.astype(BF16), (((1,), (1,)), ((), ())),
                           preferred_element_type=F32)


def _mm_tn(a, b):
    return lax.dot_general(a.astype(BF16), b.astype(BF16), (((0,), (0,)), ((), ())),
                           preferred_element_type=F32)


def _mm_sel(sel, x):
    sel = sel.astype(BF16)
    x1 = x.astype(BF16)
    x2 = (x - x1.astype(F32)).astype(BF16)
    dot = lambda t: jnp.dot(sel, t, preferred_element_type=F32)
    return dot(x1) + dot(x2)


def _bdot(a, b):
    return jnp.dot(a.astype(BF16), b, preferred_element_type=F32)


def _inproj_body(x_ref, g_ref, w_ref, o_ref):
    h = _rms(x_ref[...], g_ref[...])
    o_ref[...] = _bdot(h, w_ref[...])


def _inproj(x, g, w):
    rows = x.shape[0]
    tm = min(rows, 256)
    return pl.pallas_call(
        _inproj_body,
        grid=(rows // tm,),
        in_specs=[
            pl.BlockSpec((tm, D_MODEL), lambda i: (i, 0)),
            pl.BlockSpec((1, D_MODEL), lambda i: (0, 0)),
            pl.BlockSpec((D_MODEL, P_COLS), lambda i: (0, 0)),
        ],
        out_specs=pl.BlockSpec((tm, P_COLS), lambda i: (i, 0)),
        out_shape=jax.ShapeDtypeStruct((rows, P_COLS), F32),
        compiler_params=pltpu.CompilerParams(
            dimension_semantics=("parallel",), vmem_limit_bytes=VMEM_LIMIT),
        name="inproj",
    )(x, g, w)


def _row_prep(proj, halo_qkv, halo_rw, cw, mu, rows):
    H = SUBLANES
    R = rows
    x_qkv = proj[:, C_QKV:C_QKV + DN_QKV]
    ext = jnp.concatenate([halo_qkv, x_qkv], axis=0)
    back = lambda a, k: pltpu.roll(a, k, 0)[H:H + R, :]
    conv = (back(ext, 3) * cw[0:1, :] + back(ext, 2) * cw[1:2, :] + back(ext, 1) * cw[2:3, :]
            + x_qkv * cw[3:4, :])
    qkv = conv * _sigmoid(conv)

    def l2n(x):
        return x * lax.rsqrt(jnp.sum(x * x, axis=-1, keepdims=True) + L2_EPS)

    hl = lambda base, h: slice(base + h * LANES, base + (h + 1) * LANES)
    parts = ([l2n(qkv[:, hl(0, h)]) * (DN_HEAD_DIM ** -0.5) for h in range(DN_HEADS)]
             + [l2n(qkv[:, hl(DN_WIDTH, h)]) for h in range(DN_HEADS)]
             + [qkv[:, hl(2 * DN_WIDTH, h)] for h in range(DN_HEADS)])
    cols = proj[:, C_RW:C_RW + RW_COLS]
    prev = back(jnp.concatenate([halo_rw, cols], axis=0), 1)
    xs = cols + (prev - cols) * mu
    return parts, xs, x_qkv[R - H:R, :], cols[R - H:R, :]


def _inproj_prep_body(x_ref, g_ref, w_ref, conv0_ref, shift0_ref, cw_ref, mu_ref,
                      o_ref, ctail_ref, stail_ref, halo_qkv, halo_rw, *, rows):
    @pl.when(pl.program_id(1) == 0)
    def _():
        halo_qkv[...] = conv0_ref[0]
        halo_rw[...] = shift0_ref[0]

    proj = _bdot(_rms(x_ref[...], g_ref[...]), w_ref[...])
    parts, xs, tail_qkv, tail_rw = _row_prep(proj, halo_qkv[...], halo_rw[...], cw_ref[...],
                                             mu_ref[...], rows)
    for idx, part in enumerate(parts):
        o_ref[:, idx * LANES:(idx + 1) * LANES] = part
    o_ref[:, C_Z:C_RW] = proj[:, C_Z:C_RW]
    o_ref[:, C_RW:C_RW + RW_COLS] = xs
    halo_qkv[...] = tail_qkv
    halo_rw[...] = tail_rw
    ctail_ref[0] = tail_qkv
    stail_ref[0] = tail_rw


def _inproj_prep(x, g, w, conv0, shift0, lp, *, batch, seq, rows):
    nt = seq // rows
    bsel = (lambda b: b) if conv0.shape[0] == batch else (lambda b: 0)
    full = lambda shape: pl.BlockSpec(shape, lambda b, t: (0,) * len(shape))
    return pl.pallas_call(
        functools.partial(_inproj_prep_body, rows=rows),
        grid=(batch, nt),
        in_specs=[
            pl.BlockSpec((rows, D_MODEL), lambda b, t: (b * nt + t, 0)),
            full((1, D_MODEL)), full((D_MODEL, P_COLS)),
            pl.BlockSpec((1, SUBLANES, DN_QKV), lambda b, t: (bsel(b), 0, 0)),
            pl.BlockSpec((1, SUBLANES, RW_COLS), lambda b, t: (bsel(b), 0, 0)),
            full((SUBLANES, DN_QKV)), full((1, RW_COLS)),
        ],
        out_specs=[
            pl.BlockSpec((rows, P_COLS), lambda b, t: (b * nt + t, 0)),
            pl.BlockSpec((1, SUBLANES, DN_QKV), lambda b, t: (b, 0, 0)),
            pl.BlockSpec((1, SUBLANES, RW_COLS), lambda b, t: (b, 0, 0)),
        ],
        out_shape=[jax.ShapeDtypeStruct((batch * seq, P_COLS), F32),
                   jax.ShapeDtypeStruct((batch, SUBLANES, DN_QKV), F32),
                   jax.ShapeDtypeStruct((batch, SUBLANES, RW_COLS), F32)],
        scratch_shapes=[pltpu.VMEM((SUBLANES, DN_QKV), F32), pltpu.VMEM((SUBLANES, RW_COLS), F32)],
        compiler_params=pltpu.CompilerParams(
            dimension_semantics=("parallel", "arbitrary"), vmem_limit_bytes=VMEM_LIMIT),
        name="inproj_prep",
    )(x, g, w, conv0, shift0, lp["conv_w"], lp["mu"])


FF_CHUNK = 256


def _post_body(x_ref, m_ref, wo_ref, g1_ref, g2_ref, wg_ref, wu_ref, wd_ref, g3_ref, o_ref):
    mixed = _bdot(m_ref[...], wo_ref[...])
    x1 = x_ref[...] + _rms(mixed, g1_ref[...])
    h = _rms(x1, g2_ref[...]).astype(BF16)
    acc = jnp.zeros(x1.shape, F32)
    for c in range(D_FF // FF_CHUNK):
        sl = slice(c * FF_CHUNK, (c + 1) * FF_CHUNK)
        gate = jnp.dot(h, wg_ref[:, sl], preferred_element_type=F32)
        up = jnp.dot(h, wu_ref[:, sl], preferred_element_type=F32)
        act = gate * _sigmoid(gate) * up
        acc = acc + _bdot(act, wd_ref[sl, :])
    o_ref[...] = x1 + _rms(acc, g3_ref[...])


def _post(x, mixed, wo, g1, g2, wg, wu, wd, g3):
    rows = x.shape[0]
    tm = min(rows, 512)
    row_spec = pl.BlockSpec((tm, D_MODEL), lambda i: (i, 0))
    vec_spec = pl.BlockSpec((1, D_MODEL), lambda i: (0, 0))

    def resident(shape):
        return pl.BlockSpec(shape, lambda i: (0, 0), pipeline_mode=pl.Buffered(1))

    return pl.pallas_call(
        _post_body,
        grid=(rows // tm,),
        in_specs=[row_spec, row_spec, resident((D_MODEL, D_MODEL)), vec_spec, vec_spec,
                  resident((D_MODEL, D_FF)), resident((D_MODEL, D_FF)), resident((D_FF, D_MODEL)),
                  vec_spec],
        out_specs=row_spec,
        out_shape=jax.ShapeDtypeStruct((rows, D_MODEL), F32),
        compiler_params=pltpu.CompilerParams(
            dimension_semantics=("parallel",), vmem_limit_bytes=VMEM_LIMIT),
        name="post",
    )(x, mixed, wo, g1, g2, wg, wu, wd, g3)


def _inv_unit_lower_many(mats, eye, chunk):
    ns = [-a for a in mats]
    ps = [eye + n for n in ns]
    for _ in range(max(int(math.ceil(math.log2(chunk))) - 1, 0)):
        ns = [_mm(n, n) for n in ns]
        ps = [p + _mm(p, n) for p, n in zip(ps, ns)]
    return ps


def _stack(*parts):
    return jnp.concatenate(parts, axis=0)


def _pair_blockdiag(s0, s1):
    zero = jnp.zeros_like(s0)
    return jnp.concatenate([jnp.concatenate([s0, zero], axis=1),
                            jnp.concatenate([zero, s1], axis=1)], axis=0)


def _mixer_body(proj_ref, *refs, rows, group, chunk, n_pad, carry):
    if carry:
        (dn0_ref, rw0_ref, convw_ref, gdnp_ref, dnnorm_ref, mu_ref, rwvec_ref, wup_ref, aup_ref, gup_ref,
         mixed_ref, dn_out_ref, rw_out_ref, *scratch) = refs
    else:
        (hconv_ref, hshift_ref, dn0_ref, rw0_ref, convw_ref, gdnp_ref, dnnorm_ref, mu_ref, rwvec_ref,
         wup_ref, aup_ref, gup_ref, mixed_ref, dn_out_ref, rw_out_ref, *scratch) = refs
    R, GS, C = rows, group, chunk
    NG = R // GS
    NS = GS // C
    S2 = 2 * GS
    H = SUBLANES
    shift = int(math.log2(C))
    t = pl.program_id(1)

    D = RW_HEAD_DIM
    if carry:
        dn_s, rw_s = scratch

        @pl.when(t == 0)
        def _():
            dn_s[...] = dn0_ref[0]
            for p in range(RW_HEADS // 2):
                rw_s[p] = _pair_blockdiag(rw0_ref[0, 2 * p], rw0_ref[0, 2 * p + 1])

    ri = lax.broadcasted_iota(jnp.int32, (S2, S2), 0)
    ci = lax.broadcasted_iota(jnp.int32, (S2, S2), 1)
    same = (ri >> shift) == (ci >> shift)
    strict = same & (ci < ri)
    incl = same & (ci <= ri)
    eye2 = (ri == ci).astype(F32)
    r1 = lax.broadcasted_iota(jnp.int32, (GS, GS), 0)
    c1 = lax.broadcasted_iota(jnp.int32, (GS, GS), 1)
    lt_blk = (((r1 >> shift) == (c1 >> shift)) & (c1 <= r1)).astype(F32)
    last_sel = (c1 == (((r1 >> shift) << shift) + (C - 1))).astype(F32)
    lane = lax.broadcasted_iota(jnp.int32, (1, LANES), 1)
    m0 = (lane < RW_HEAD_DIM).astype(F32)
    m1 = 1.0 - m0
    bi = lax.broadcasted_iota(jnp.int32, (LANES, LANES), 0)
    bj = lax.broadcasted_iota(jnp.int32, (LANES, LANES), 1)
    bd = ((bi >= RW_HEAD_DIM) == (bj >= RW_HEAD_DIM)).astype(F32)
    if n_pad:
        rowm = ((lax.broadcasted_iota(jnp.int32, (R, 1), 0) & (C - 1)) >= n_pad).astype(F32)
    else:
        rowm = None
    grp = [slice(g * GS, (g + 1) * GS) for g in range(NG)]

    def block_last(x):
        if NS == 1:
            return jnp.broadcast_to(x[GS - 1:GS, :], x.shape)
        return _mm_sel(last_sel, x)

    hl = lambda base, h: slice(base + h * LANES, base + (h + 1) * LANES)
    if carry:
        qn = [proj_ref[:, hl(0, h)] for h in range(DN_HEADS)]
        kn = [proj_ref[:, hl(DN_WIDTH, h)] for h in range(DN_HEADS)]
        vv = [proj_ref[:, hl(2 * DN_WIDTH, h)] for h in range(DN_HEADS)]
        xs = proj_ref[:, C_RW:C_RW + RW_COLS]
        z = proj_ref[:, C_Z:C_Z + DN_WIDTH]
        bg = proj_ref[:, C_BG:C_BG + LANES]
        put = lambda lanes, val: mixed_ref.__setitem__((slice(None), lanes), val)
    else:
        n_real = C - n_pad
        pr = lax.broadcasted_iota(jnp.int32, (R, R * n_real // C), 0)
        rr = lax.broadcasted_iota(jnp.int32, (R, R * n_real // C), 1)
        same_seq = (pr >> shift) == (rr >> int(math.log2(n_real)))
        e_tok = (same_seq & ((pr & (C - 1)) - n_pad == (rr & (n_real - 1)))).astype(F32)
        e_hist = (same_seq & ((pr & (C - 1)) == (rr & (n_real - 1)))).astype(F32)
        pc = lax.broadcasted_iota(jnp.int32, (R * n_real // C, R), 1)
        rc = lax.broadcasted_iota(jnp.int32, (R * n_real // C, R), 0)
        e_back = (((pc >> shift) == (rc >> int(math.log2(n_real))))
                  & ((pc & (C - 1)) - n_pad == (rc & (n_real - 1)))).astype(F32)
        tile_qkv = (_mm_sel(e_tok, proj_ref[:, C_QKV:C_QKV + DN_QKV]) + _mm_sel(e_hist, hconv_ref[...]))
        tile_rw = (_mm_sel(e_tok, proj_ref[:, C_RW:C_RW + RW_COLS]) + _mm_sel(e_hist, hshift_ref[...]))
        zbg = _mm_sel(e_tok, proj_ref[:, C_Z:C_RW])
        z = zbg[:, :DN_WIDTH]
        bg = zbg[:, DN_WIDTH:]
        parts, xs, _, _ = _row_prep(jnp.concatenate([tile_qkv, zbg, tile_rw], axis=1),
                                    jnp.zeros((H, DN_QKV), F32), jnp.zeros((H, RW_COLS), F32),
                                    convw_ref[...], mu_ref[...], R)
        qn, kn, vv = parts[:DN_HEADS], parts[DN_HEADS:2 * DN_HEADS], parts[2 * DN_HEADS:]
        put = lambda lanes, val: mixed_ref.__setitem__((slice(None), lanes), _mm_sel(e_back, val))

    gp = gdnp_ref[...]
    beta_all = _sigmoid(bg)
    glog_all = -jnp.exp(gp[1:2, :]) * _softplus(bg + gp[0:1, :])
    if rowm is not None:
        beta_all = beta_all * rowm
        glog_all = glog_all * rowm
    gcum = [_mm_sel(lt_blk, glog_all[rs]) for rs in grp]
    glast = [block_last(gc) for gc in gcum]

    r = xs[:, 0:RW_WIDTH]
    kr = xs[:, RW_WIDTH:2 * RW_WIDTH]
    vr = xs[:, 2 * RW_WIDTH:3 * RW_WIDTH]
    wa = xs[:, 3 * RW_WIDTH:3 * RW_WIDTH + LANES]
    gd = xs[:, 3 * RW_WIDTH + LANES:RW_COLS]
    rv = rwvec_ref[...]
    w0, a0, k_k, k_a, r_k, ln_w, ln_b = (rv[i:i + 1, :] for i in range(7))
    w_log = -_softplus(-(w0 + _mm(jnp.tanh(wa), wup_ref[...]))) - 0.5
    lw = -jnp.exp(w_log)
    a = _sigmoid(a0 + _mm(wa, aup_ref[...]))
    g = _mm(_sigmoid(gd), gup_ref[...])
    kk_raw = kr * k_k
    kr2 = kr * (1.0 + (a - 1.0) * k_a)
    bonus_in = r * kr2 * r_k
    if rowm is not None:
        lw = lw * rowm
        kr2 = kr2 * rowm
    cl_g = [_mm_sel(lt_blk, lw[rs]) for rs in grp]
    cl = cl_g[0] if NG == 1 else jnp.concatenate(cl_g, axis=0)
    p_last = [jnp.exp(block_last(c)) for c in cl_g]
    e_cl = jnp.exp(cl)
    e_prev = jnp.exp(cl - lw)
    e_inv = jnp.exp(-cl)
    r_t = r * e_cl
    k_t = kr2 * e_inv
    pl_ = lambda p: slice(p * LANES, (p + 1) * LANES)
    kk = []
    for p in range(RW_HEADS // 2):
        kp = kk_raw[:, pl_(p)]
        kp = kp * lax.rsqrt(_mm(kp * kp, bd) + L2_EPS)
        if rowm is not None:
            kp = kp * rowm
        kk.append(kp)

    gd_keys = [(gi, p) for gi in range(NG) for p in range(DN_HEADS // 2)]

    def gstack(arrs, gi, p):
        return _stack(arrs[2 * p][grp[gi]], arrs[2 * p + 1][grp[gi]])

    def gcol(mat, col, p):
        return _stack(mat[:, col + 2 * p:col + 2 * p + 1], mat[:, col + 2 * p + 1:col + 2 * p + 2])

    q2 = [gstack(qn, gi, p) for gi, p in gd_keys]
    k2 = [gstack(kn, gi, p) for gi, p in gd_keys]
    v2 = [gstack(vv, gi, p) for gi, p in gd_keys]
    beta2 = [gcol(beta_all[grp[gi]], 0, p) for gi, p in gd_keys]
    gc2 = [gcol(gcum[gi], DN_HEADS, p) for gi, p in gd_keys]
    gl2 = [gcol(glast[gi], DN_HEADS, p) for gi, p in gd_keys]
    gct = [jnp.transpose(gc) for gc in gcum]
    rb = [jnp.concatenate([gct[gi][DN_HEADS + 2 * p:DN_HEADS + 2 * p + 1, :],
                           gct[gi][DN_HEADS + 2 * p + 1:DN_HEADS + 2 * p + 2, :]], axis=1)
          for gi, p in gd_keys]
    dec = [jnp.exp(jnp.where(incl, c - b, 0.0)) for c, b in zip(gc2, rb)]
    kb = [k * b for k, b in zip(k2, beta2)]
    s_kk = [_mm_nt(x, k) for x, k in zip(kb, k2)]
    s_qk = [_mm_nt(x, k) for x, k in zip(q2, k2)]
    a_mat = [jnp.where(strict, s * d, 0.0) for s, d in zip(s_kk, dec)]
    qk = [jnp.where(incl, s * d, 0.0) for s, d in zip(s_qk, dec)]
    t_dn = _inv_unit_lower_many(a_mat, eye2, C)
    eg = [jnp.exp(c) for c in gc2]
    uw = [_mm(tm, jnp.concatenate([v * b, x * e], axis=1))
          for tm, v, b, x, e in zip(t_dn, v2, beta2, kb, eg)]
    qo = [_mm(m, x) for m, x in zip(qk, uw)]
    u_dn = [x[:, :LANES] for x in uw]
    w_dn = [x[:, LANES:] for x in uw]
    o2_dn = [x[:, :LANES] for x in qo]
    qg2 = [q * e - x[:, LANES:] for q, e, x in zip(q2, eg, qo)]
    k_dec = [k * jnp.exp(l - c) for k, l, c in zip(k2, gl2, gc2)]
    e_gl = [jnp.exp(l) for l in gl2]

    rw_keys = [(gi, p) for gi in range(NG) for p in range(RW_HEADS // 2)]
    kk_t = [kk[p][grp[gi]] * e_prev[grp[gi], pl_(p)] for gi, p in rw_keys]
    a_t = [kk[p][grp[gi]] * a[grp[gi], pl_(p)] * e_inv[grp[gi], pl_(p)] for gi, p in rw_keys]
    rt_p = [r_t[grp[gi], pl_(p)] for gi, p in rw_keys]
    kt_p = [k_t[grp[gi], pl_(p)] for gi, p in rw_keys]
    v_p = [vr[grp[gi], pl_(p)] for gi, p in rw_keys]
    l_kk = [_stack(x * m0, x * m1) for x in kk_t]
    l_r = [_stack(x * m0, x * m1) for x in rt_p]
    vs = [_stack(x * m0, x * m1) for x in v_p]
    if S2 % LANES == 0:
        sc = [_mm_nt(_stack(lk, lr), _stack(x, x, y, y))
              for lk, lr, x, y in zip(l_kk, l_r, a_t, kt_p)]
        m_ka = [jnp.where(strict, s[:S2, :S2], 0.0) for s in sc]
        m_kv = [jnp.where(strict, s[:S2, S2:], 0.0) for s in sc]
        a_ra = [jnp.where(incl, s[S2:, :S2], 0.0) for s in sc]
        a_rk = [jnp.where(incl, s[S2:, S2:], 0.0) for s in sc]
    else:
        ra2 = [_stack(x, x) for x in a_t]
        rk2 = [_stack(x, x) for x in kt_p]
        m_ka = [jnp.where(strict, _mm_nt(x, y), 0.0) for x, y in zip(l_kk, ra2)]
        m_kv = [jnp.where(strict, _mm_nt(x, y), 0.0) for x, y in zip(l_kk, rk2)]
        a_ra = [jnp.where(incl, _mm_nt(x, y), 0.0) for x, y in zip(l_r, ra2)]
        a_rk = [jnp.where(incl, _mm_nt(x, y), 0.0) for x, y in zip(l_r, rk2)]
    t_rw = _inv_unit_lower_many(m_ka, eye2, C)
    mv = [_mm(m, x) for m, x in zip(m_kv, vs)]
    tlw = [_mm(tm, jnp.concatenate([lk, x], axis=1)) for tm, lk, x in zip(t_rw, l_kk, mv)]
    aa = [_mm(m, x) for m, x in zip(a_ra, tlw)]
    av = [_mm(m, x) for m, x in zip(a_rk, vs)]
    tl = [x[:, :LANES] for x in tlw]
    w2 = [x[:, LANES:] for x in tlw]
    ql = [lr - x[:, :LANES] for lr, x in zip(l_r, aa)]
    y2 = [x - y[:, LANES:] for x, y in zip(av, aa)]

    o_rows = [[None] * NG for _ in range(DN_HEADS)]
    y_rows = [[None] * NG for _ in range(RW_HEADS // 2)]
    if carry:
        s_dn = [dn_s[h] for h in range(DN_HEADS)]
        s_rw = [rw_s[p] for p in range(RW_HEADS // 2)]
        for gi in range(NG):
            hsl = lambda e: slice(e * GS, (e + 1) * GS)
            dn_i = [(gd_keys.index((gi, h // 2)), h % 2) for h in range(DN_HEADS)]
            rw_i = [rw_keys.index((gi, p)) for p in range(RW_HEADS // 2)]
            ss_dn = [_mm(_stack(w_dn[i][hsl(e)], qg2[i][hsl(e)]), s_dn[h])
                     for h, (i, e) in enumerate(dn_i)]
            ss_rw = [_mm_nt(_stack(tl[i], ql[i]), s_rw[p]) for p, i in enumerate(rw_i)]
            v_new = [u_dn[i][hsl(e)] - ss[:GS] for ss, (i, e) in zip(ss_dn, dn_i)]
            for h, (ss, (i, e)) in enumerate(zip(ss_dn, dn_i)):
                o_rows[h][gi] = ss[GS:] + o2_dn[i][hsl(e)]
            us = [ss[:S2] + w2[i] for ss, i in zip(ss_rw, rw_i)]
            for p, (ss, i) in enumerate(zip(ss_rw, rw_i)):
                ys = ss[S2:] + y2[i]
                y_rows[p][gi] = ys[:GS] + ys[GS:]
            u1 = [x[:GS] + x[GS:] for x in us]
            s_dn = [s * e_gl[i][e * GS:e * GS + 1, :] + _mm_tn(k_dec[i][hsl(e)], vn)
                    for s, vn, (i, e) in zip(s_dn, v_new, dn_i)]
            s_rw = [(s + _mm_tn(_stack(v_p[i], -x), _stack(kt_p[i], a_t[i]))) * bd
                    * p_last[gi][0:1, pl_(p)]
                    for p, (s, x, i) in enumerate(zip(s_rw, u1, rw_i))]
        for h in range(DN_HEADS):
            dn_s[h] = s_dn[h]
        for p in range(RW_HEADS // 2):
            rw_s[p] = s_rw[p]

        @pl.when(t == pl.num_programs(1) - 1)
        def _():
            dn_out_ref[0] = dn_s[...]
            for p in range(RW_HEADS // 2):
                rw_out_ref[0, 2 * p] = rw_s[p][:D, :D]
                rw_out_ref[0, 2 * p + 1] = rw_s[p][D:, D:]
    else:
        for gi in range(NG):
            seqs = range(NS)
            rsl = lambda e, j: slice(e * GS + j * C, e * GS + (j + 1) * C)
            dn_items = [(j, h, gd_keys.index((gi, h // 2)), h % 2) for j in seqs for h in range(DN_HEADS)]
            rw_items = [(j, p, rw_keys.index((gi, p))) for j in seqs for p in range(RW_HEADS // 2)]
            s_dn = [dn0_ref[gi * NS + j, h] for j, h, _, _ in dn_items]
            s_rw = [_pair_blockdiag(rw0_ref[gi * NS + j, 2 * p], rw0_ref[gi * NS + j, 2 * p + 1])
                    for j, p, _ in rw_items]
            ss_dn = [_mm(_stack(w_dn[i][rsl(e, j)], qg2[i][rsl(e, j)]), s)
                     for s, (j, h, i, e) in zip(s_dn, dn_items)]
            ss_rw = [_mm_nt(_stack(tl[i][rsl(0, j)], tl[i][rsl(1, j)], ql[i][rsl(0, j)], ql[i][rsl(1, j)]), s)
                     for s, (j, p, i) in zip(s_rw, rw_items)]
            v_new = [u_dn[i][rsl(e, j)] - ss[:C] for ss, (j, h, i, e) in zip(ss_dn, dn_items)]
            o_seq = [ss[C:] + o2_dn[i][rsl(e, j)] for ss, (j, h, i, e) in zip(ss_dn, dn_items)]
            us = [ss[:2 * C] + _stack(w2[i][rsl(0, j)], w2[i][rsl(1, j)]) for ss, (j, p, i) in zip(ss_rw, rw_items)]
            ysq = [ss[2 * C:] + _stack(y2[i][rsl(0, j)], y2[i][rsl(1, j)]) for ss, (j, p, i) in zip(ss_rw, rw_items)]
            u1 = [x[:C] + x[C:] for x in us]
            y_seq = [x[:C] + x[C:] for x in ysq]
            for s, vn, (j, h, i, e) in zip(s_dn, v_new, dn_items):
                row = e * GS + j * C
                dn_out_ref[gi * NS + j, h] = (s * e_gl[i][row:row + 1, :]
                                             + _mm_tn(k_dec[i][rsl(e, j)], vn))
            for s, x, (j, p, i) in zip(s_rw, u1, rw_items):
                sq = slice(j * C, (j + 1) * C)
                s_new = ((s + _mm_tn(_stack(v_p[i][sq], -x), _stack(kt_p[i][sq], a_t[i][sq])))
                         * bd * p_last[gi][j * C:j * C + 1, pl_(p)])
                rw_out_ref[gi * NS + j, 2 * p] = s_new[:D, :D]
                rw_out_ref[gi * NS + j, 2 * p + 1] = s_new[D:, D:]
            for h in range(DN_HEADS):
                parts = [o for o, (j, hh, _, _) in zip(o_seq, dn_items) if hh == h]
                o_rows[h][gi] = parts[0] if NS == 1 else jnp.concatenate(parts, axis=0)
            for p in range(RW_HEADS // 2):
                parts = [y for y, (j, pp, _) in zip(y_seq, rw_items) if pp == p]
                y_rows[p][gi] = parts[0] if NS == 1 else jnp.concatenate(parts, axis=0)

    cat = lambda parts: parts[0] if len(parts) == 1 else jnp.concatenate(parts, axis=0)
    dn_w = dnnorm_ref[...]
    for h in range(DN_HEADS):
        zh = z[:, h * LANES:(h + 1) * LANES]
        put(slice(h * LANES, (h + 1) * LANES), _rms(cat(o_rows[h]), dn_w) * (zh * _sigmoid(zh)))
    for p in range(RW_HEADS // 2):
        sl = pl_(p)
        y = cat(y_rows[p])
        mu_g = _mm(y, bd) * (1.0 / RW_HEAD_DIM)
        yc = y - mu_g
        var = _mm(yc * yc, bd) * (1.0 / RW_HEAD_DIM)
        yn = yc * lax.rsqrt(var + GN_EPS) * ln_w[:, sl] + ln_b[:, sl]
        bonus = _mm(bonus_in[:, sl], bd) * vr[:, sl]
        put(slice(DN_WIDTH + p * LANES, DN_WIDTH + (p + 1) * LANES), (yn + bonus) * g[:, sl])


def _mixer(proj, dn0, rw0, lp, *, batch, seq, rows, group, chunk, n_pad, carry, layer=None, hist=()):
    full = lambda shape: pl.BlockSpec(shape, lambda b, t: (0,) * len(shape))
    dn_blk = (DN_HEADS, DN_HEAD_DIM, DN_HEAD_DIM)
    rw_blk = (RW_HEADS, RW_HEAD_DIM, RW_HEAD_DIM)
    if carry:
        nt = seq // rows
        grid = (batch, nt)
        bsel = (lambda b: b) if dn0.shape[0] == batch else (lambda b: 0)
        row_map = lambda b, t: (b * nt + t, 0)
        st_in = [pl.BlockSpec((1,) + blk, lambda b, t: (bsel(b), 0, 0, 0)) for blk in (dn_blk, rw_blk)]
        st_out = [pl.BlockSpec((1,) + blk, lambda b, t: (b, 0, 0, 0)) for blk in (dn_blk, rw_blk)]
        scratch = [pltpu.VMEM((DN_HEADS, LANES, LANES), F32), pltpu.VMEM((RW_HEADS // 2, LANES, LANES), F32)]
        io_rows = rows
    else:
        nseq = rows // chunk
        grid = (batch // nseq, 1)
        row_map = lambda b, t: (b, 0)
        io_rows = nseq * (chunk - n_pad)
        st_in = [pl.BlockSpec((None, nseq) + blk, lambda b, t: (layer, b, 0, 0, 0)) for blk in (dn_blk, rw_blk)]
        st_out = [pl.BlockSpec((nseq,) + blk, lambda b, t: (b, 0, 0, 0)) for blk in (dn_blk, rw_blk)]
        scratch = []
    return pl.pallas_call(
        functools.partial(_mixer_body, rows=rows, group=group, chunk=chunk, n_pad=n_pad, carry=carry),
        grid=grid,
        in_specs=[
            pl.BlockSpec((io_rows, P_COLS), row_map),
            *[pl.BlockSpec((io_rows, h.shape[1]), row_map) for h in hist],
            *st_in,
            full((SUBLANES, DN_QKV)), full((SUBLANES, LANES)), full((1, LANES)),
            full((1, RW_COLS)), full((SUBLANES, RW_WIDTH)),
            full((LANES, RW_WIDTH)), full((LANES, RW_WIDTH)), full((G_LORA, RW_WIDTH)),
        ],
        out_specs=[pl.BlockSpec((io_rows, D_MODEL), row_map), *st_out],
        out_shape=[jax.ShapeDtypeStruct((proj.shape[0], D_MODEL), F32),
                   jax.ShapeDtypeStruct((batch,) + dn_blk, F32),
                   jax.ShapeDtypeStruct((batch,) + rw_blk, F32)],
        scratch_shapes=scratch,
        compiler_params=pltpu.CompilerParams(
            dimension_semantics=("parallel", "arbitrary"), vmem_limit_bytes=VMEM_LIMIT),
        name="mixer",
    )(proj, *hist, dn0, rw0, lp["conv_w"], lp["gdnp"], lp["dn_norm"], lp["mu"],
      lp["rwvec"], lp["w_up"], lp["a_up"], lp["g_up"])


def _prep_layer_params(l, w_in, dn_conv_w, dn_a_log, dn_dt_bias, dn_norm_w, rw_mu, rw_w0, rw_w_up,
                       rw_a0, rw_a_up, rw_g_up, rw_k_k, rw_k_a, rw_r_k, rw_ln_w, rw_ln_b):
    off = DN_QKV + DN_WIDTH
    wi = w_in[l]
    w_in_p = jnp.concatenate(
        [wi[:, :off + 2 * DN_HEADS], jnp.zeros((D_MODEL, LANES - 2 * DN_HEADS), F32),
         wi[:, off + 2 * DN_HEADS:]], axis=1).astype(BF16)
    gdnp = jnp.zeros((SUBLANES, LANES), F32)
    gdnp = gdnp.at[0, DN_HEADS:2 * DN_HEADS].set(dn_dt_bias[l])
    gdnp = gdnp.at[1, DN_HEADS:2 * DN_HEADS].set(dn_a_log[l])
    rwvec = jnp.stack([rw_w0[l], rw_a0[l], rw_k_k[l], rw_k_a[l], rw_r_k[l].reshape(RW_WIDTH),
                       rw_ln_w[l], rw_ln_b[l], jnp.zeros((RW_WIDTH,), F32)])
    zeros_lora = jnp.zeros((W_LORA, RW_WIDTH), F32)
    return {
        "w_in": w_in_p,
        "conv_w": jnp.concatenate([dn_conv_w[l], jnp.zeros((SUBLANES - CONV_W, DN_QKV), F32)], axis=0),
        "gdnp": gdnp,
        "dn_norm": dn_norm_w[l].reshape(1, LANES),
        "mu": rw_mu[l].reshape(1, RW_COLS),
        "rwvec": rwvec,
        "w_up": jnp.concatenate([rw_w_up[l], zeros_lora], axis=0),
        "a_up": jnp.concatenate([zeros_lora, rw_a_up[l]], axis=0),
        "g_up": rw_g_up[l],
    }


def _post_layer(x, mixed, ffn, l):
    return _post(x, mixed, ffn["w_out"][l], ffn["g_post_mix"][l], ffn["g_pre_ffn"][l],
                 ffn["w_gate"][l], ffn["w_up"][l], ffn["w_down"][l], ffn["g_post_ffn"][l])


def _trunk_carry(x, states, layers, ffn, *, batch, seq, rows, group):
    new_states = []
    for l in range(DEPTH):
        conv_s, dn_s, shift_s, rw_s = states[l]
        prep, conv_t, shift_t = _inproj_prep(x, ffn["g_pre_mix"][l], layers[l]["w_in"], conv_s, shift_s,
                                             layers[l], batch=batch, seq=seq, rows=rows)
        mixed, dn_new, rw_new = _mixer(prep, dn_s, rw_s, layers[l], batch=batch, seq=seq, rows=rows,
                                       group=group, chunk=group, n_pad=0, carry=True)
        x = _post_layer(x, mixed, ffn, l)
        new_states.append((conv_t, dn_new, shift_t, rw_new))
    return x, new_states


def _trunk_sample(x, state_dn_conv, state_dn, state_rw_shift, state_rw, layers, ffn, *, batch, seq_real):
    seq = SUBLANES
    n_pad = seq - seq_real
    assert n_pad == seq_real and n_pad >= CONV_W - 1, "history rows are laid out like the token rows"
    front = lambda a: jnp.pad(a, ((0, 0), (seq_real - a.shape[1], 0), (0, 0))).reshape(batch * seq_real, -1)
    new_states = []
    for l in range(DEPTH):
        proj = _inproj(x, ffn["g_pre_mix"][l], layers[l]["w_in"])
        hist = (front(state_dn_conv[l]), front(state_rw_shift[l][:, None, :]))
        mixed, dn_new, rw_new = _mixer(proj, state_dn, state_rw, layers[l], batch=batch, seq=seq,
                                       rows=SAMPLE_ROWS, group=GROUP, chunk=seq, n_pad=n_pad, carry=False,
                                       layer=l, hist=hist)
        x = _post_layer(x, mixed, ffn, l)
        proj3 = proj.reshape(batch, seq_real, P_COLS)
        conv_new = proj3[:, seq_real - (CONV_W - 1):, C_QKV:C_QKV + DN_QKV]
        shift_new = proj3[:, seq_real - 1, C_RW:]
        new_states.append((conv_new, dn_new, shift_new, rw_new))
    return x, new_states


def kernel(x_prompt, x_sample, state_dn_conv, state_dn, state_rw_shift, state_rw, meta, g_pre_mix, g_post_mix, g_pre_ffn, g_post_ffn, w_in, dn_conv_w, dn_a_log, dn_dt_bias, dn_norm_w, rw_mu, rw_w0, rw_w_up, rw_a0, rw_a_up, rw_g_up, rw_k_k, rw_k_a, rw_r_k, rw_ln_w, rw_ln_b, w_out, w_gate, w_up, w_down):
    batch, seq, _ = x_prompt.shape
    dec_batch, dec_seq, _ = x_sample.shape
    layers = [_prep_layer_params(l, w_in, dn_conv_w, dn_a_log, dn_dt_bias, dn_norm_w, rw_mu, rw_w0,
                                 rw_w_up, rw_a0, rw_a_up, rw_g_up, rw_k_k, rw_k_a, rw_r_k, rw_ln_w,
                                 rw_ln_b) for l in range(DEPTH)]
    vec = lambda g: g.reshape(DEPTH, 1, D_MODEL)
    ffn = {"g_pre_mix": vec(g_pre_mix), "g_post_mix": vec(g_post_mix), "g_pre_ffn": vec(g_pre_ffn),
           "g_post_ffn": vec(g_post_ffn), "w_out": w_out.astype(BF16), "w_gate": w_gate.astype(BF16),
           "w_up": w_up.astype(BF16), "w_down": w_down.astype(BF16)}

    zero_states = [(jnp.zeros((1, SUBLANES, DN_QKV), F32), jnp.zeros((1, DN_HEADS, DN_HEAD_DIM, DN_HEAD_DIM), F32),
                    jnp.zeros((1, SUBLANES, RW_COLS), F32),
                    jnp.zeros((1, RW_HEADS, RW_HEAD_DIM, RW_HEAD_DIM), F32))] * DEPTH
    _, meta_states = _trunk_carry(meta.astype(F32), zero_states, layers, ffn,
                                  batch=1, seq=N_META, rows=N_META, group=N_META)

    yp, ps = _trunk_carry(x_prompt.reshape(batch * seq, D_MODEL), meta_states, layers, ffn,
                          batch=batch, seq=seq, rows=PROMPT_ROWS, group=GROUP)
    ps = [(c[:, SUBLANES - (CONV_W - 1):], d, s[:, SUBLANES - 1], r) for c, d, s, r in ps]

    ys, ss = _trunk_sample(x_sample.reshape(dec_batch * dec_seq, D_MODEL), state_dn_conv, state_dn,
                           state_rw_shift, state_rw, layers, ffn, batch=dec_batch, seq_real=dec_seq)

    def collect(sts):
        return tuple(jnp.stack([s[i] for s in sts]) for i in range(4))

    cp, dp, sp, rp = collect(ps)
    cs, ds, sh, rs = collect(ss)
    return (yp.reshape(batch, seq, D_MODEL), ys.reshape(dec_batch, dec_seq, D_MODEL),
            cp, dp, sp, rp, cs, ds, sh, rs)
```

```python
import functools
import math

import jax
import jax.numpy as jnp
from jax import lax
from jax.experimental import pallas as pl
from jax.experimental.pallas import tpu as pltpu

F32 = jnp.float32
BF16 = jnp.bfloat16

D_MODEL = 1024
DEPTH = 4
N_META = 16
DN_HEADS = 4
DN_HEAD_DIM = 128
DN_WIDTH = DN_HEADS * DN_HEAD_DIM
DN_QKV = 3 * DN_WIDTH
RW_HEADS = 8
RW_HEAD_DIM = 64
RW_WIDTH = RW_HEADS * RW_HEAD_DIM
W_LORA = 64
A_LORA = 64
G_LORA = 128
RW_COLS = 3 * RW_WIDTH + W_LORA + A_LORA + G_LORA
D_FF = 2816
CONV_W = 4
RMS_EPS = 1e-6
L2_EPS = 1e-6
GN_EPS = 64e-5

LANES = 128
SUBLANES = 8
VMEM_LIMIT = 56 * 1024 * 1024

C_QKV = 0
C_Z = DN_QKV
C_BG = C_Z + DN_WIDTH
C_RW = C_BG + LANES
P_COLS = C_RW + RW_COLS
GROUP = 64
PROMPT_ROWS = 512
SAMPLE_ROWS = 128


def _sigmoid(x):
    return 1.0 / (1.0 + jnp.exp(-x))


def _softplus(x):
    return jnp.maximum(x, 0.0) + jnp.log(1.0 + jnp.exp(-jnp.abs(x)))


def _rms(x, g):
    return x * lax.rsqrt(jnp.mean(x * x, axis=-1, keepdims=True) + RMS_EPS) * g


def _mm(a, b):
    return jnp.dot(a.astype(BF16), b.astype(BF16), preferred_element_type=F32)


def _mm_nt(a, b):
    return lax.dot_general(a.astype(BF16), b.astype(BF16), (((1,), (1,)), ((), ())),
                           preferred_element_type=F32)


def _mm_tn(a, b):
    return lax.dot_general(a.astype(BF16), b.astype(BF16), (((0,), (0,)), ((), ())),
                           preferred_element_type=F32)


def _mm_sel(sel, x):
    sel = sel.astype(BF16)
    x1 = x.astype(BF16)
    x2 = (x - x1.astype(F32)).astype(BF16)
    dot = lambda t: jnp.dot(sel, t, preferred_element_type=F32)
    return dot(x1) + dot(x2)


def _bdot(a, b):
    return jnp.dot(a.astype(BF16), b, preferred_element_type=F32)


def _inproj_body(x_ref, g_ref, w_ref, o_ref):
    h = _rms(x_ref[...], g_ref[...])
    o_ref[...] = _bdot(h, w_ref[...])


def _inproj(x, g, w):
    rows = x.shape[0]
    tm = min(rows, 256)
    return pl.pallas_call(
        _inproj_body,
        grid=(rows // tm,),
        in_specs=[
            pl.BlockSpec((tm, D_MODEL), lambda i: (i, 0)),
            pl.BlockSpec((1, D_MODEL), lambda i: (0, 0)),
            pl.BlockSpec((D_MODEL, P_COLS), lambda i: (0, 0)),
        ],
        out_specs=pl.BlockSpec((tm, P_COLS), lambda i: (i, 0)),
        out_shape=jax.ShapeDtypeStruct((rows, P_COLS), F32),
        compiler_params=pltpu.CompilerParams(
            dimension_semantics=("parallel",), vmem_limit_bytes=VMEM_LIMIT),
        name="inproj",
    )(x, g, w)


def _row_prep(proj, halo_qkv, halo_rw, cw, mu, rows):
    H = SUBLANES
    R = rows
    x_qkv = proj[:, C_QKV:C_QKV + DN_QKV]
    ext = jnp.concatenate([halo_qkv, x_qkv], axis=0)
    back = lambda a, k: pltpu.roll(a, k, 0)[H:H + R, :]
    conv = (back(ext, 3) * cw[0:1, :] + back(ext, 2) * cw[1:2, :] + back(ext, 1) * cw[2:3, :]
            + x_qkv * cw[3:4, :])
    half = 0.5 * conv
    qkv = half + half * jnp.tanh(half)

    def l2n(x):
        return x * lax.rsqrt(jnp.sum(x * x, axis=-1, keepdims=True) + L2_EPS)

    hl = lambda base, h: slice(base + h * LANES, base + (h + 1) * LANES)
    parts = ([l2n(qkv[:, hl(0, h)]) * (DN_HEAD_DIM ** -0.5) for h in range(DN_HEADS)]
             + [l2n(qkv[:, hl(DN_WIDTH, h)]) for h in range(DN_HEADS)]
             + [qkv[:, hl(2 * DN_WIDTH, h)] for h in range(DN_HEADS)])
    cols = proj[:, C_RW:C_RW + RW_COLS]
    prev = back(jnp.concatenate([halo_rw, cols], axis=0), 1)
    xs = cols + (prev - cols) * mu
    return parts, xs, x_qkv[R - H:R, :], cols[R - H:R, :]


def _inproj_prep_body(x_ref, g_ref, w_ref, conv0_ref, shift0_ref, cw_ref, mu_ref,
                      o_ref, ctail_ref, stail_ref, halo_qkv, halo_rw, *, rows):
    @pl.when(pl.program_id(1) == 0)
    def _():
        halo_qkv[...] = conv0_ref[0]
        halo_rw[...] = shift0_ref[0]

    proj = _bdot(_rms(x_ref[...], g_ref[...]), w_ref[...])
    parts, xs, tail_qkv, tail_rw = _row_prep(proj, halo_qkv[...], halo_rw[...], cw_ref[...],
                                             mu_ref[...], rows)
    for idx, part in enumerate(parts):
        o_ref[:, idx * LANES:(idx + 1) * LANES] = part
    o_ref[:, C_Z:C_RW] = proj[:, C_Z:C_RW]
    o_ref[:, C_RW:C_RW + RW_COLS] = xs
    halo_qkv[...] = tail_qkv
    halo_rw[...] = tail_rw
    ctail_ref[0] = tail_qkv
    stail_ref[0] = tail_rw


def _inproj_prep(x, g, w, conv0, shift0, lp, *, batch, seq, rows):
    nt = seq // rows
    bsel = (lambda b: b) if conv0.shape[0] == batch else (lambda b: 0)
    full = lambda shape: pl.BlockSpec(shape, lambda b, t: (0,) * len(shape))
    return pl.pallas_call(
        functools.partial(_inproj_prep_body, rows=rows),
        grid=(batch, nt),
        in_specs=[
            pl.BlockSpec((rows, D_MODEL), lambda b, t: (b * nt + t, 0)),
            full((1, D_MODEL)), full((D_MODEL, P_COLS)),
            pl.BlockSpec((1, SUBLANES, DN_QKV), lambda b, t: (bsel(b), 0, 0)),
            pl.BlockSpec((1, SUBLANES, RW_COLS), lambda b, t: (bsel(b), 0, 0)),
            full((SUBLANES, DN_QKV)), full((1, RW_COLS)),
        ],
        out_specs=[
            pl.BlockSpec((rows, P_COLS), lambda b, t: (b * nt + t, 0)),
            pl.BlockSpec((1, SUBLANES, DN_QKV), lambda b, t: (b, 0, 0)),
            pl.BlockSpec((1, SUBLANES, RW_COLS), lambda b, t: (b, 0, 0)),
        ],
        out_shape=[jax.ShapeDtypeStruct((batch * seq, P_COLS), F32),
                   jax.ShapeDtypeStruct((batch, SUBLANES, DN_QKV), F32),
                   jax.ShapeDtypeStruct((batch, SUBLANES, RW_COLS), F32)],
        scratch_shapes=[pltpu.VMEM((SUBLANES, DN_QKV), F32), pltpu.VMEM((SUBLANES, RW_COLS), F32)],
        compiler_params=pltpu.CompilerParams(
            dimension_semantics=("parallel", "arbitrary"), vmem_limit_bytes=VMEM_LIMIT),
        name="inproj_prep",
    )(x, g, w, conv0, shift0, lp["conv_w"], lp["mu"])


FF_CHUNK = 256


def _post_body(x_ref, m_ref, wo_ref, g1_ref, g2_ref, wg_ref, wu_ref, wd_ref, g3_ref, o_ref):
    mixed = _bdot(m_ref[...], wo_ref[...])
    x1 = x_ref[...] + _rms(mixed, g1_ref[...])
    h = _rms(x1, g2_ref[...]).astype(BF16)
    acc = jnp.zeros(x1.shape, F32)
    for c in range(D_FF // FF_CHUNK):
        sl = slice(c * FF_CHUNK, (c + 1) * FF_CHUNK)
        gate = jnp.dot(h, wg_ref[:, sl], preferred_element_type=F32)
        up = jnp.dot(h, wu_ref[:, sl], preferred_element_type=F32)
        act = gate * _sigmoid(gate) * up
        acc = acc + _bdot(act, wd_ref[sl, :])
    o_ref[...] = x1 + _rms(acc, g3_ref[...])


def _post(x, mixed, wo, g1, g2, wg, wu, wd, g3):
    rows = x.shape[0]
    tm = min(rows, 512)
    row_spec = pl.BlockSpec((tm, D_MODEL), lambda i: (i, 0))
    vec_spec = pl.BlockSpec((1, D_MODEL), lambda i: (0, 0))

    def resident(shape):
        return pl.BlockSpec(shape, lambda i: (0, 0), pipeline_mode=pl.Buffered(1))

    return pl.pallas_call(
        _post_body,
        grid=(rows // tm,),
        in_specs=[row_spec, row_spec, resident((D_MODEL, D_MODEL)), vec_spec, vec_spec,
                  resident((D_MODEL, D_FF)), resident((D_MODEL, D_FF)), resident((D_FF, D_MODEL)),
                  vec_spec],
        out_specs=row_spec,
        out_shape=jax.ShapeDtypeStruct((rows, D_MODEL), F32),
        compiler_params=pltpu.CompilerParams(
            dimension_semantics=("parallel",), vmem_limit_bytes=VMEM_LIMIT),
        name="post",
    )(x, mixed, wo, g1, g2, wg, wu, wd, g3)


def _inv_unit_lower_many(mats, eye, chunk):
    ns = [-a for a in mats]
    ps = [eye + n for n in ns]
    for _ in range(max(int(math.ceil(math.log2(chunk))) - 1, 0)):
        ns = [_mm(n, n) for n in ns]
        ps = [p + _mm(p, n) for p, n in zip(ps, ns)]
    return ps


def _stack(*parts):
    return jnp.concatenate(parts, axis=0)


def _pair_blockdiag(s0, s1):
    zero = jnp.zeros_like(s0)
    return jnp.concatenate([jnp.concatenate([s0, zero], axis=1),
                            jnp.concatenate([zero, s1], axis=1)], axis=0)


def _mixer_body(proj_ref, *refs, rows, group, chunk, n_pad, carry):
    if carry:
        (dn0_ref, rw0_ref, convw_ref, gdnp_ref, dnnorm_ref, mu_ref, rwvec_ref, wup_ref, aup_ref, gup_ref,
         mixed_ref, dn_out_ref, rw_out_ref, *scratch) = refs
    else:
        (hconv_ref, hshift_ref, dn0_ref, rw0_ref, _, _, convw_ref, gdnp_ref, dnnorm_ref, mu_ref, rwvec_ref,
         wup_ref, aup_ref, gup_ref, mixed_ref, dn_out_ref, rw_out_ref, *scratch) = refs
    R, GS, C = rows, group, chunk
    NG = R // GS
    NS = GS // C
    S2 = 2 * GS
    H = SUBLANES
    shift = int(math.log2(C))
    t = pl.program_id(1)

    D = RW_HEAD_DIM
    if carry:
        dn_s, rw_s = scratch

        @pl.when(t == 0)
        def _():
            dn_s[...] = dn0_ref[0]
            for p in range(RW_HEADS // 2):
                rw_s[p] = _pair_blockdiag(rw0_ref[0, 2 * p], rw0_ref[0, 2 * p + 1])

    ri = lax.broadcasted_iota(jnp.int32, (S2, S2), 0)
    ci = lax.broadcasted_iota(jnp.int32, (S2, S2), 1)
    same = (ri >> shift) == (ci >> shift)
    strict = same & (ci < ri)
    incl = same & (ci <= ri)
    eye2 = (ri == ci).astype(F32)
    r1 = lax.broadcasted_iota(jnp.int32, (GS, GS), 0)
    c1 = lax.broadcasted_iota(jnp.int32, (GS, GS), 1)
    lt_blk = (((r1 >> shift) == (c1 >> shift)) & (c1 <= r1)).astype(F32)
    last_sel = (c1 == (((r1 >> shift) << shift) + (C - 1))).astype(F32)
    lane = lax.broadcasted_iota(jnp.int32, (1, LANES), 1)
    m0 = (lane < RW_HEAD_DIM).astype(F32)
    m1 = 1.0 - m0
    bi = lax.broadcasted_iota(jnp.int32, (LANES, LANES), 0)
    bj = lax.broadcasted_iota(jnp.int32, (LANES, LANES), 1)
    bd = ((bi >= RW_HEAD_DIM) == (bj >= RW_HEAD_DIM)).astype(F32)
    if n_pad:
        rowm = ((lax.broadcasted_iota(jnp.int32, (R, 1), 0) & (C - 1)) >= n_pad).astype(F32)
    else:
        rowm = None
    grp = [slice(g * GS, (g + 1) * GS) for g in range(NG)]

    def block_last(x):
        if NS == 1:
            return jnp.broadcast_to(x[GS - 1:GS, :], x.shape)
        return _mm_sel(last_sel, x)

    hl = lambda base, h: slice(base + h * LANES, base + (h + 1) * LANES)
    if carry:
        qn = [proj_ref[:, hl(0, h)] for h in range(DN_HEADS)]
        kn = [proj_ref[:, hl(DN_WIDTH, h)] for h in range(DN_HEADS)]
        vv = [proj_ref[:, hl(2 * DN_WIDTH, h)] for h in range(DN_HEADS)]
        xs = proj_ref[:, C_RW:C_RW + RW_COLS]
        z = proj_ref[:, C_Z:C_Z + DN_WIDTH]
        bg = proj_ref[:, C_BG:C_BG + LANES]
        put = lambda lanes, val: mixed_ref.__setitem__((slice(None), lanes), val)
    else:
        n_real = C - n_pad
        pr = lax.broadcasted_iota(jnp.int32, (R, R * n_real // C), 0)
        rr = lax.broadcasted_iota(jnp.int32, (R, R * n_real // C), 1)
        same_seq = (pr >> shift) == (rr >> int(math.log2(n_real)))
        e_tok = (same_seq & ((pr & (C - 1)) - n_pad == (rr & (n_real - 1)))).astype(F32)
        e_hist = (same_seq & ((pr & (C - 1)) == (rr & (n_real - 1)))).astype(F32)
        pc = lax.broadcasted_iota(jnp.int32, (R * n_real // C, R), 1)
        rc = lax.broadcasted_iota(jnp.int32, (R * n_real // C, R), 0)
        e_back = (((pc >> shift) == (rc >> int(math.log2(n_real))))
                  & ((pc & (C - 1)) - n_pad == (rc & (n_real - 1)))).astype(F32)
        tile_qkv = (_mm_sel(e_tok, proj_ref[:, C_QKV:C_QKV + DN_QKV]) + _mm_sel(e_hist, hconv_ref[...]))
        tile_rw = (_mm_sel(e_tok, proj_ref[:, C_RW:C_RW + RW_COLS]) + _mm_sel(e_hist, hshift_ref[...]))
        zbg = _mm_sel(e_tok, proj_ref[:, C_Z:C_RW])
        z = zbg[:, :DN_WIDTH]
        bg = zbg[:, DN_WIDTH:]
        parts, xs, _, _ = _row_prep(jnp.concatenate([tile_qkv, zbg, tile_rw], axis=1),
                                    jnp.zeros((H, DN_QKV), F32), jnp.zeros((H, RW_COLS), F32),
                                    convw_ref[...], mu_ref[...], R)
        qn, kn, vv = parts[:DN_HEADS], parts[DN_HEADS:2 * DN_HEADS], parts[2 * DN_HEADS:]
        put = lambda lanes, val: mixed_ref.__setitem__((slice(None), lanes), _mm_sel(e_back, val))

    gp = gdnp_ref[...]
    beta_all = _sigmoid(bg)
    glog_all = -jnp.exp(gp[1:2, :]) * _softplus(bg + gp[0:1, :])
    if rowm is not None:
        beta_all = beta_all * rowm
        glog_all = glog_all * rowm
    gcum = [_mm_sel(lt_blk, glog_all[rs]) for rs in grp]
    glast = [block_last(gc) for gc in gcum]

    r = xs[:, 0:RW_WIDTH]
    kr = xs[:, RW_WIDTH:2 * RW_WIDTH]
    vr = xs[:, 2 * RW_WIDTH:3 * RW_WIDTH]
    wa = xs[:, 3 * RW_WIDTH:3 * RW_WIDTH + LANES]
    gd = xs[:, 3 * RW_WIDTH + LANES:RW_COLS]
    rv = rwvec_ref[...]
    w0, a0, k_k, k_a, r_k, ln_w, ln_b = (rv[i:i + 1, :] for i in range(7))
    w_log = -_softplus(-(w0 + _mm(jnp.tanh(wa), wup_ref[...]))) - 0.5
    lw = -jnp.exp(w_log)
    a = _sigmoid(a0 + _mm(wa, aup_ref[...]))
    g = _mm(_sigmoid(gd), gup_ref[...])
    kk_raw = kr * k_k
    kr2 = kr * (1.0 + (a - 1.0) * k_a)
    bonus_in = r * kr2 * r_k
    if rowm is not None:
        lw = lw * rowm
        kr2 = kr2 * rowm
    cl_g = [_mm_sel(lt_blk, lw[rs]) for rs in grp]
    cl = cl_g[0] if NG == 1 else jnp.concatenate(cl_g, axis=0)
    p_last = [jnp.exp(block_last(c)) for c in cl_g]
    e_cl = jnp.exp(cl)
    e_prev = jnp.exp(cl - lw)
    e_inv = jnp.exp(-cl)
    r_t = r * e_cl
    k_t = kr2 * e_inv
    pl_ = lambda p: slice(p * LANES, (p + 1) * LANES)
    kk = []
    for p in range(RW_HEADS // 2):
        kp = kk_raw[:, pl_(p)]
        kp = kp * lax.rsqrt(_mm(kp * kp, bd) + L2_EPS)
        if rowm is not None:
            kp = kp * rowm
        kk.append(kp)

    gd_keys = [(gi, p) for gi in range(NG) for p in range(DN_HEADS // 2)]

    def gstack(arrs, gi, p):
        return _stack(arrs[2 * p][grp[gi]], arrs[2 * p + 1][grp[gi]])

    def gcol(mat, col, p):
        return _stack(mat[:, col + 2 * p:col + 2 * p + 1], mat[:, col + 2 * p + 1:col + 2 * p + 2])

    q2 = [gstack(qn, gi, p) for gi, p in gd_keys]
    k2 = [gstack(kn, gi, p) for gi, p in gd_keys]
    v2 = [gstack(vv, gi, p) for gi, p in gd_keys]
    beta2 = [gcol(beta_all[grp[gi]], 0, p) for gi, p in gd_keys]
    gc2 = [gcol(gcum[gi], DN_HEADS, p) for gi, p in gd_keys]
    gl2 = [gcol(glast[gi], DN_HEADS, p) for gi, p in gd_keys]
    gct = [jnp.transpose(gc) for gc in gcum]
    rb = [jnp.concatenate([gct[gi][DN_HEADS + 2 * p:DN_HEADS + 2 * p + 1, :],
                           gct[gi][DN_HEADS + 2 * p + 1:DN_HEADS + 2 * p + 2, :]], axis=1)
          for gi, p in gd_keys]
    dec = [jnp.exp(jnp.where(incl, c - b, 0.0)) for c, b in zip(gc2, rb)]
    kb = [k * b for k, b in zip(k2, beta2)]
    s_kk = [_mm_nt(x, k) for x, k in zip(kb, k2)]
    s_qk = [_mm_nt(x, k) for x, k in zip(q2, k2)]
    a_mat = [jnp.where(strict, s * d, 0.0) for s, d in zip(s_kk, dec)]
    qk = [jnp.where(incl, s * d, 0.0) for s, d in zip(s_qk, dec)]
    t_dn = _inv_unit_lower_many(a_mat, eye2, C)
    eg = [jnp.exp(c) for c in gc2]
    uw = [_mm(tm, jnp.concatenate([v * b, x * e], axis=1))
          for tm, v, b, x, e in zip(t_dn, v2, beta2, kb, eg)]
    qo = [_mm(m, x) for m, x in zip(qk, uw)]
    u_dn = [x[:, :LANES] for x in uw]
    w_dn = [x[:, LANES:] for x in uw]
    o2_dn = [x[:, :LANES] for x in qo]
    qg2 = [q * e - x[:, LANES:] for q, e, x in zip(q2, eg, qo)]
    k_dec = [k * jnp.exp(l - c) for k, l, c in zip(k2, gl2, gc2)]
    e_gl = [jnp.exp(l) for l in gl2]

    rw_keys = [(gi, p) for gi in range(NG) for p in range(RW_HEADS // 2)]
    kk_t = [kk[p][grp[gi]] * e_prev[grp[gi], pl_(p)] for gi, p in rw_keys]
    a_t = [kk[p][grp[gi]] * a[grp[gi], pl_(p)] * e_inv[grp[gi], pl_(p)] for gi, p in rw_keys]
    rt_p = [r_t[grp[gi], pl_(p)] for gi, p in rw_keys]
    kt_p = [k_t[grp[gi], pl_(p)] for gi, p in rw_keys]
    v_p = [vr[grp[gi], pl_(p)] for gi, p in rw_keys]
    l_kk = [_stack(x * m0, x * m1) for x in kk_t]
    l_r = [_stack(x * m0, x * m1) for x in rt_p]
    vs = [_stack(x * m0, x * m1) for x in v_p]
    if S2 % LANES == 0:
        sc = [_mm_nt(_stack(lk, lr), _stack(x, x, y, y))
              for lk, lr, x, y in zip(l_kk, l_r, a_t, kt_p)]
        m_ka = [jnp.where(strict, s[:S2, :S2], 0.0) for s in sc]
        m_kv = [jnp.where(strict, s[:S2, S2:], 0.0) for s in sc]
        a_ra = [jnp.where(incl, s[S2:, :S2], 0.0) for s in sc]
        a_rk = [jnp.where(incl, s[S2:, S2:], 0.0) for s in sc]
    else:
        ra2 = [_stack(x, x) for x in a_t]
        rk2 = [_stack(x, x) for x in kt_p]
        m_ka = [jnp.where(strict, _mm_nt(x, y), 0.0) for x, y in zip(l_kk, ra2)]
        m_kv = [jnp.where(strict, _mm_nt(x, y), 0.0) for x, y in zip(l_kk, rk2)]
        a_ra = [jnp.where(incl, _mm_nt(x, y), 0.0) for x, y in zip(l_r, ra2)]
        a_rk = [jnp.where(incl, _mm_nt(x, y), 0.0) for x, y in zip(l_r, rk2)]
    t_rw = _inv_unit_lower_many(m_ka, eye2, C)
    mv = [_mm(m, x) for m, x in zip(m_kv, vs)]
    tlw = [_mm(tm, jnp.concatenate([lk, x], axis=1)) for tm, lk, x in zip(t_rw, l_kk, mv)]
    aa = [_mm(m, x) for m, x in zip(a_ra, tlw)]
    av = [_mm(m, x) for m, x in zip(a_rk, vs)]
    tl = [x[:, :LANES] for x in tlw]
    w2 = [x[:, LANES:] for x in tlw]
    ql = [lr - x[:, :LANES] for lr, x in zip(l_r, aa)]
    y2 = [x - y[:, LANES:] for x, y in zip(av, aa)]

    o_rows = [[None] * NG for _ in range(DN_HEADS)]
    y_rows = [[None] * NG for _ in range(RW_HEADS // 2)]
    if carry:
        s_dn = [dn_s[h] for h in range(DN_HEADS)]
        s_rw = [rw_s[p] for p in range(RW_HEADS // 2)]
        for gi in range(NG):
            hsl = lambda e: slice(e * GS, (e + 1) * GS)
            dn_i = [(gd_keys.index((gi, h // 2)), h % 2) for h in range(DN_HEADS)]
            rw_i = [rw_keys.index((gi, p)) for p in range(RW_HEADS // 2)]
            ss_dn = [_mm(_stack(w_dn[i][hsl(e)], qg2[i][hsl(e)]), s_dn[h])
                     for h, (i, e) in enumerate(dn_i)]
            ss_rw = [_mm_nt(_stack(tl[i], ql[i]), s_rw[p]) for p, i in enumerate(rw_i)]
            v_new = [u_dn[i][hsl(e)] - ss[:GS] for ss, (i, e) in zip(ss_dn, dn_i)]
            for h, (ss, (i, e)) in enumerate(zip(ss_dn, dn_i)):
                o_rows[h][gi] = ss[GS:] + o2_dn[i][hsl(e)]
            us = [ss[:S2] + w2[i] for ss, i in zip(ss_rw, rw_i)]
            for p, (ss, i) in enumerate(zip(ss_rw, rw_i)):
                ys = ss[S2:] + y2[i]
                y_rows[p][gi] = ys[:GS] + ys[GS:]
            u1 = [x[:GS] + x[GS:] for x in us]
            s_dn = [s * e_gl[i][e * GS:e * GS + 1, :] + _mm_tn(k_dec[i][hsl(e)], vn)
                    for s, vn, (i, e) in zip(s_dn, v_new, dn_i)]
            s_rw = [(s + _mm_tn(_stack(v_p[i], -x), _stack(kt_p[i], a_t[i]))) * bd
                    * p_last[gi][0:1, pl_(p)]
                    for p, (s, x, i) in enumerate(zip(s_rw, u1, rw_i))]
        for h in range(DN_HEADS):
            dn_s[h] = s_dn[h]
        for p in range(RW_HEADS // 2):
            rw_s[p] = s_rw[p]

        @pl.when(t == pl.num_programs(1) - 1)
        def _():
            dn_out_ref[0] = dn_s[...]
            for p in range(RW_HEADS // 2):
                rw_out_ref[0, 2 * p] = rw_s[p][:D, :D]
                rw_out_ref[0, 2 * p + 1] = rw_s[p][D:, D:]
    else:
        for gi in range(NG):
            seqs = range(NS)
            rsl = lambda e, j: slice(e * GS + j * C, e * GS + (j + 1) * C)
            dn_items = [(j, h, gd_keys.index((gi, h // 2)), h % 2) for j in seqs for h in range(DN_HEADS)]
            rw_items = [(j, p, rw_keys.index((gi, p))) for j in seqs for p in range(RW_HEADS // 2)]
            s_dn = [dn0_ref[gi * NS + j, h] for j, h, _, _ in dn_items]
            s_rw = [_pair_blockdiag(rw0_ref[gi * NS + j, 2 * p], rw0_ref[gi * NS + j, 2 * p + 1])
                    for j, p, _ in rw_items]
            ss_dn = [_mm(_stack(w_dn[i][rsl(e, j)], qg2[i][rsl(e, j)]), s)
                     for s, (j, h, i, e) in zip(s_dn, dn_items)]
            ss_rw = [_mm_nt(_stack(tl[i][rsl(0, j)], tl[i][rsl(1, j)], ql[i][rsl(0, j)], ql[i][rsl(1, j)]), s)
                     for s, (j, p, i) in zip(s_rw, rw_items)]
            v_new = [u_dn[i][rsl(e, j)] - ss[:C] for ss, (j, h, i, e) in zip(ss_dn, dn_items)]
            o_seq = [ss[C:] + o2_dn[i][rsl(e, j)] for ss, (j, h, i, e) in zip(ss_dn, dn_items)]
            us = [ss[:2 * C] + _stack(w2[i][rsl(0, j)], w2[i][rsl(1, j)]) for ss, (j, p, i) in zip(ss_rw, rw_items)]
            ysq = [ss[2 * C:] + _stack(y2[i][rsl(0, j)], y2[i][rsl(1, j)]) for ss, (j, p, i) in zip(ss_rw, rw_items)]
            u1 = [x[:C] + x[C:] for x in us]
            y_seq = [x[:C] + x[C:] for x in ysq]
            for s, vn, (j, h, i, e) in zip(s_dn, v_new, dn_items):
                row = e * GS + j * C
                dn_out_ref[gi * NS + j, h] = (s * e_gl[i][row:row + 1, :]
                                             + _mm_tn(k_dec[i][rsl(e, j)], vn))
            for s, x, (j, p, i) in zip(s_rw, u1, rw_items):
                sq = slice(j * C, (j + 1) * C)
                s_new = ((s + _mm_tn(_stack(v_p[i][sq], -x), _stack(kt_p[i][sq], a_t[i][sq])))
                         * bd * p_last[gi][j * C:j * C + 1, pl_(p)])
                rw_out_ref[gi * NS + j, 2 * p] = s_new[:D, :D]
                rw_out_ref[gi * NS + j, 2 * p + 1] = s_new[D:, D:]
            for h in range(DN_HEADS):
                parts = [o for o, (j, hh, _, _) in zip(o_seq, dn_items) if hh == h]
                o_rows[h][gi] = parts[0] if NS == 1 else jnp.concatenate(parts, axis=0)
            for p in range(RW_HEADS // 2):
                parts = [y for y, (j, pp, _) in zip(y_seq, rw_items) if pp == p]
                y_rows[p][gi] = parts[0] if NS == 1 else jnp.concatenate(parts, axis=0)

    cat = lambda parts: parts[0] if len(parts) == 1 else jnp.concatenate(parts, axis=0)
    dn_w = dnnorm_ref[...]
    for h in range(DN_HEADS):
        zh = z[:, h * LANES:(h + 1) * LANES]
        put(slice(h * LANES, (h + 1) * LANES), _rms(cat(o_rows[h]), dn_w) * (zh * _sigmoid(zh)))
    for p in range(RW_HEADS // 2):
        sl = pl_(p)
        y = cat(y_rows[p])
        mu_g = _mm(y, bd) * (1.0 / RW_HEAD_DIM)
        yc = y - mu_g
        var = _mm(yc * yc, bd) * (1.0 / RW_HEAD_DIM)
        yn = yc * lax.rsqrt(var + GN_EPS) * ln_w[:, sl] + ln_b[:, sl]
        bonus = _mm(bonus_in[:, sl], bd) * vr[:, sl]
        put(slice(DN_WIDTH + p * LANES, DN_WIDTH + (p + 1) * LANES), (yn + bonus) * g[:, sl])


def _mixer(proj, dn0, rw0, lp, *, batch, seq, rows, group, chunk, n_pad, carry, layer=None, hist=(),
           acc=()):
    full = lambda shape: pl.BlockSpec(shape, lambda b, t: (0,) * len(shape))
    dn_blk = (DN_HEADS, DN_HEAD_DIM, DN_HEAD_DIM)
    rw_blk = (RW_HEADS, RW_HEAD_DIM, RW_HEAD_DIM)
    if carry:
        nt = seq // rows
        grid = (batch, nt)
        bsel = (lambda b: b) if dn0.shape[0] == batch else (lambda b: 0)
        row_map = lambda b, t: (b * nt + t, 0)
        st_in = [pl.BlockSpec((1,) + blk, lambda b, t: (bsel(b), 0, 0, 0)) for blk in (dn_blk, rw_blk)]
        st_out = [pl.BlockSpec((1,) + blk, lambda b, t: (b, 0, 0, 0)) for blk in (dn_blk, rw_blk)]
        scratch = [pltpu.VMEM((DN_HEADS, LANES, LANES), F32), pltpu.VMEM((RW_HEADS // 2, LANES, LANES), F32)]
        io_rows = rows
    else:
        nseq = rows // chunk
        grid = (batch // nseq, 1)
        row_map = lambda b, t: (b, 0)
        io_rows = nseq * (chunk - n_pad)
        st_in = [pl.BlockSpec((None, nseq) + blk, lambda b, t: (layer, b, 0, 0, 0)) for blk in (dn_blk, rw_blk)]
        st_in += [pl.BlockSpec(memory_space=pl.ANY)] * len(acc)
        st_out = [pl.BlockSpec((None, nseq) + blk, lambda b, t: (layer, b, 0, 0, 0)) for blk in (dn_blk, rw_blk)]
        scratch = []
    state_shapes = ([jax.ShapeDtypeStruct(a.shape, F32) for a in acc] if acc else
                    [jax.ShapeDtypeStruct((batch,) + blk, F32) for blk in (dn_blk, rw_blk)])
    first_acc = 1 + len(hist) + 2
    return pl.pallas_call(
        functools.partial(_mixer_body, rows=rows, group=group, chunk=chunk, n_pad=n_pad, carry=carry),
        grid=grid,
        in_specs=[
            pl.BlockSpec((io_rows, P_COLS), row_map),
            *[pl.BlockSpec((io_rows, h.shape[1]), row_map) for h in hist],
            *st_in,
            full((SUBLANES, DN_QKV)), full((SUBLANES, LANES)), full((1, LANES)),
            full((1, RW_COLS)), full((SUBLANES, RW_WIDTH)),
            full((LANES, RW_WIDTH)), full((LANES, RW_WIDTH)), full((G_LORA, RW_WIDTH)),
        ],
        out_specs=[pl.BlockSpec((io_rows, D_MODEL), row_map), *st_out],
        out_shape=[jax.ShapeDtypeStruct((proj.shape[0], D_MODEL), F32), *state_shapes],
        input_output_aliases={first_acc + i: 1 + i for i in range(len(acc))},
        scratch_shapes=scratch,
        compiler_params=pltpu.CompilerParams(
            dimension_semantics=("parallel", "arbitrary"), vmem_limit_bytes=VMEM_LIMIT),
        name="mixer",
    )(proj, *hist, dn0, rw0, *acc, lp["conv_w"], lp["gdnp"], lp["dn_norm"], lp["mu"],
      lp["rwvec"], lp["w_up"], lp["a_up"], lp["g_up"])


def _prep_layer_params(l, w_in, dn_conv_w, dn_a_log, dn_dt_bias, dn_norm_w, rw_mu, rw_w0, rw_w_up,
                       rw_a0, rw_a_up, rw_g_up, rw_k_k, rw_k_a, rw_r_k, rw_ln_w, rw_ln_b):
    off = DN_QKV + DN_WIDTH
    wi = w_in[l]
    w_in_p = jnp.concatenate(
        [wi[:, :off + 2 * DN_HEADS], jnp.zeros((D_MODEL, LANES - 2 * DN_HEADS), F32),
         wi[:, off + 2 * DN_HEADS:]], axis=1).astype(BF16)
    gdnp = jnp.zeros((SUBLANES, LANES), F32)
    gdnp = gdnp.at[0, DN_HEADS:2 * DN_HEADS].set(dn_dt_bias[l])
    gdnp = gdnp.at[1, DN_HEADS:2 * DN_HEADS].set(dn_a_log[l])
    rwvec = jnp.stack([rw_w0[l], rw_a0[l], rw_k_k[l], rw_k_a[l], rw_r_k[l].reshape(RW_WIDTH),
                       rw_ln_w[l], rw_ln_b[l], jnp.zeros((RW_WIDTH,), F32)])
    zeros_lora = jnp.zeros((W_LORA, RW_WIDTH), F32)
    return {
        "w_in": w_in_p,
        "conv_w": jnp.concatenate([dn_conv_w[l], jnp.zeros((SUBLANES - CONV_W, DN_QKV), F32)], axis=0),
        "gdnp": gdnp,
        "dn_norm": dn_norm_w[l].reshape(1, LANES),
        "mu": rw_mu[l].reshape(1, RW_COLS),
        "rwvec": rwvec,
        "w_up": jnp.concatenate([rw_w_up[l], zeros_lora], axis=0),
        "a_up": jnp.concatenate([zeros_lora, rw_a_up[l]], axis=0),
        "g_up": rw_g_up[l],
    }


def _post_layer(x, mixed, ffn, l):
    return _post(x, mixed, ffn["w_out"][l], ffn["g_post_mix"][l], ffn["g_pre_ffn"][l],
                 ffn["w_gate"][l], ffn["w_up"][l], ffn["w_down"][l], ffn["g_post_ffn"][l])


def _trunk_carry(x, states, layers, ffn, *, batch, seq, rows, group):
    new_states = []
    for l in range(DEPTH):
        conv_s, dn_s, shift_s, rw_s = states[l]
        prep, conv_t, shift_t = _inproj_prep(x, ffn["g_pre_mix"][l], layers[l]["w_in"], conv_s, shift_s,
                                             layers[l], batch=batch, seq=seq, rows=rows)
        mixed, dn_new, rw_new = _mixer(prep, dn_s, rw_s, layers[l], batch=batch, seq=seq, rows=rows,
                                       group=group, chunk=group, n_pad=0, carry=True)
        x = _post_layer(x, mixed, ffn, l)
        new_states.append((conv_t, dn_new, shift_t, rw_new))
    return x, new_states


def _trunk_sample(x, state_dn_conv, state_dn, state_rw_shift, state_rw, layers, ffn, *, batch, seq_real):
    seq = SUBLANES
    n_pad = seq - seq_real
    assert n_pad == seq_real and n_pad >= CONV_W - 1, "history rows are laid out like the token rows"
    front = lambda a: jnp.pad(a, ((0, 0), (seq_real - a.shape[1], 0), (0, 0))).reshape(batch * seq_real, -1)
    conv_new, shift_new = [], []
    dn_acc = jnp.zeros(state_dn.shape, F32)
    rw_acc = jnp.zeros(state_rw.shape, F32)
    for l in range(DEPTH):
        proj = _inproj(x, ffn["g_pre_mix"][l], layers[l]["w_in"])
        hist = (front(state_dn_conv[l]), front(state_rw_shift[l][:, None, :]))
        mixed, dn_acc, rw_acc = _mixer(proj, state_dn, state_rw, layers[l], batch=batch, seq=seq,
                                       rows=SAMPLE_ROWS, group=GROUP, chunk=seq, n_pad=n_pad, carry=False,
                                       layer=l, hist=hist, acc=(dn_acc, rw_acc))
        x = _post_layer(x, mixed, ffn, l)
        conv_new.append(jnp.stack([proj[t::seq_real, C_QKV:C_QKV + DN_QKV]
                                   for t in range(seq_real - (CONV_W - 1), seq_real)], axis=1))
        shift_new.append(proj[seq_real - 1::seq_real, C_RW:])
    return x, (jnp.stack(conv_new), dn_acc, jnp.stack(shift_new), rw_acc)


def kernel(x_prompt, x_sample, state_dn_conv, state_dn, state_rw_shift, state_rw, meta, g_pre_mix, g_post_mix, g_pre_ffn, g_post_ffn, w_in, dn_conv_w, dn_a_log, dn_dt_bias, dn_norm_w, rw_mu, rw_w0, rw_w_up, rw_a0, rw_a_up, rw_g_up, rw_k_k, rw_k_a, rw_r_k, rw_ln_w, rw_ln_b, w_out, w_gate, w_up, w_down):
    batch, seq, _ = x_prompt.shape
    dec_batch, dec_seq, _ = x_sample.shape
    layers = [_prep_layer_params(l, w_in, dn_conv_w, dn_a_log, dn_dt_bias, dn_norm_w, rw_mu, rw_w0,
                                 rw_w_up, rw_a0, rw_a_up, rw_g_up, rw_k_k, rw_k_a, rw_r_k, rw_ln_w,
                                 rw_ln_b) for l in range(DEPTH)]
    vec = lambda g: g.reshape(DEPTH, 1, D_MODEL)
    ffn = {"g_pre_mix": vec(g_pre_mix), "g_post_mix": vec(g_post_mix), "g_pre_ffn": vec(g_pre_ffn),
           "g_post_ffn": vec(g_post_ffn), "w_out": w_out.astype(BF16), "w_gate": w_gate.astype(BF16),
           "w_up": w_up.astype(BF16), "w_down": w_down.astype(BF16)}

    zero_states = [(jnp.zeros((1, SUBLANES, DN_QKV), F32), jnp.zeros((1, DN_HEADS, DN_HEAD_DIM, DN_HEAD_DIM), F32),
                    jnp.zeros((1, SUBLANES, RW_COLS), F32),
                    jnp.zeros((1, RW_HEADS, RW_HEAD_DIM, RW_HEAD_DIM), F32))] * DEPTH
    _, meta_states = _trunk_carry(meta.astype(F32), zero_states, layers, ffn,
                                  batch=1, seq=N_META, rows=N_META, group=N_META)

    yp, ps = _trunk_carry(x_prompt.reshape(batch * seq, D_MODEL), meta_states, layers, ffn,
                          batch=batch, seq=seq, rows=PROMPT_ROWS, group=GROUP)
    ps = [(c[:, SUBLANES - (CONV_W - 1):], d, s[:, SUBLANES - 1], r) for c, d, s, r in ps]

    ys, (cs, ds, sh, rs) = _trunk_sample(x_sample.reshape(dec_batch * dec_seq, D_MODEL), state_dn_conv,
                                         state_dn, state_rw_shift, state_rw, layers, ffn,
                                         batch=dec_batch, seq_real=dec_seq)

    cp, dp, sp, rp = (jnp.stack([s[i] for s in ps]) for i in range(4))
    return (yp.reshape(batch, seq, D_MODEL), ys.reshape(dec_batch, dec_seq, D_MODEL),
            cp, dp, sp, rp, cs, ds, sh, rs)
```

```python
import functools
import math

import jax
import jax.numpy as jnp
from jax import lax
from jax.experimental import pallas as pl
from jax.experimental.pallas import tpu as pltpu

F32 = jnp.float32
BF16 = jnp.bfloat16

D_MODEL = 1024
DEPTH = 4
N_META = 16
DN_HEADS = 4
DN_HEAD_DIM = 128
DN_WIDTH = DN_HEADS * DN_HEAD_DIM
DN_QKV = 3 * DN_WIDTH
RW_HEADS = 8
RW_HEAD_DIM = 64
RW_WIDTH = RW_HEADS * RW_HEAD_DIM
W_LORA = 64
A_LORA = 64
G_LORA = 128
RW_COLS = 3 * RW_WIDTH + W_LORA + A_LORA + G_LORA
D_FF = 2816
CONV_W = 4
RMS_EPS = 1e-6
L2_EPS = 1e-6
GN_EPS = 64e-5

LANES = 128
SUBLANES = 8
VMEM_LIMIT = 56 * 1024 * 1024

C_QKV = 0
C_Z = DN_QKV
C_BG = C_Z + DN_WIDTH
C_RW = C_BG + LANES
P_COLS = C_RW + RW_COLS
GROUP = 64
PROMPT_ROWS = 512
SAMPLE_ROWS = 128


def _sigmoid(x):
    return 1.0 / (1.0 + jnp.exp(-x))


def _softplus(x):
    return jnp.maximum(x, 0.0) + jnp.log(1.0 + jnp.exp(-jnp.abs(x)))


def _rms(x, g):
    return x * lax.rsqrt(jnp.mean(x * x, axis=-1, keepdims=True) + RMS_EPS) * g


def _mm(a, b):
    return jnp.dot(a.astype(BF16), b.astype(BF16), preferred_element_type=F32)


def _mm_nt(a, b):
    return lax.dot_general(a.astype(BF16), b.astype(BF16), (((1,), (1,)), ((), ())),
                           preferred_element_type=F32)


def _mm_tn(a, b):
    return lax.dot_general(a.astype(BF16), b.astype(BF16), (((0,), (0,)), ((), ())),
                           preferred_element_type=F32)


def _mm_sel(sel, x):
    sel = sel.astype(BF16)
    x1 = x.astype(BF16)
    x2 = (x - x1.astype(F32)).astype(BF16)
    dot = lambda t: jnp.dot(sel, t, preferred_element_type=F32)
    return dot(x1) + dot(x2)


def _bdot(a, b):
    return jnp.dot(a.astype(BF16), b, preferred_element_type=F32)


def _inproj_body(x_ref, g_ref, w_ref, o_ref):
    h = _rms(x_ref[...], g_ref[...])
    o_ref[...] = _bdot(h, w_ref[...])


def _inproj(x, g, w):
    rows = x.shape[0]
    tm = min(rows, 256)
    return pl.pallas_call(
        _inproj_body,
        grid=(rows // tm,),
        in_specs=[
            pl.BlockSpec((tm, D_MODEL), lambda i: (i, 0)),
            pl.BlockSpec((1, D_MODEL), lambda i: (0, 0)),
            pl.BlockSpec((D_MODEL, P_COLS), lambda i: (0, 0)),
        ],
        out_specs=pl.BlockSpec((tm, P_COLS), lambda i: (i, 0)),
        out_shape=jax.ShapeDtypeStruct((rows, P_COLS), F32),
        compiler_params=pltpu.CompilerParams(
            dimension_semantics=("parallel",), vmem_limit_bytes=VMEM_LIMIT),
        name="inproj",
    )(x, g, w)


def _row_prep(proj, halo_qkv, halo_rw, cw, mu, rows):
    H = SUBLANES
    R = rows
    x_qkv = proj[:, C_QKV:C_QKV + DN_QKV]
    ext = jnp.concatenate([halo_qkv, x_qkv], axis=0)
    back = lambda a, k: pltpu.roll(a, k, 0)[H:H + R, :]
    conv = (back(ext, 3) * cw[0:1, :] + back(ext, 2) * cw[1:2, :] + back(ext, 1) * cw[2:3, :]
            + x_qkv * cw[3:4, :])
    half = 0.5 * conv
    qkv = half + half * jnp.tanh(half)

    def l2n(x):
        return x * lax.rsqrt(jnp.sum(x * x, axis=-1, keepdims=True) + L2_EPS)

    hl = lambda base, h: slice(base + h * LANES, base + (h + 1) * LANES)
    parts = ([l2n(qkv[:, hl(0, h)]) * (DN_HEAD_DIM ** -0.5) for h in range(DN_HEADS)]
             + [l2n(qkv[:, hl(DN_WIDTH, h)]) for h in range(DN_HEADS)]
             + [qkv[:, hl(2 * DN_WIDTH, h)] for h in range(DN_HEADS)])
    cols = proj[:, C_RW:C_RW + RW_COLS]
    prev = back(jnp.concatenate([halo_rw, cols], axis=0), 1)
    xs = cols + (prev - cols) * mu
    return parts, xs, x_qkv[R - H:R, :], cols[R - H:R, :]


def _inproj_prep_body(x_ref, g_ref, w_ref, conv0_ref, shift0_ref, cw_ref, mu_ref,
                      o_ref, ctail_ref, stail_ref, halo_qkv, halo_rw, *, rows):
    @pl.when(pl.program_id(1) == 0)
    def _():
        halo_qkv[...] = conv0_ref[0]
        halo_rw[...] = shift0_ref[0]

    proj = _bdot(_rms(x_ref[...], g_ref[...]), w_ref[...])
    parts, xs, tail_qkv, tail_rw = _row_prep(proj, halo_qkv[...], halo_rw[...], cw_ref[...],
                                             mu_ref[...], rows)
    for idx, part in enumerate(parts):
        o_ref[:, idx * LANES:(idx + 1) * LANES] = part
    o_ref[:, C_Z:C_RW] = proj[:, C_Z:C_RW]
    o_ref[:, C_RW:C_RW + RW_COLS] = xs
    halo_qkv[...] = tail_qkv
    halo_rw[...] = tail_rw
    ctail_ref[0] = tail_qkv
    stail_ref[0] = tail_rw


def _inproj_prep(x, g, w, conv0, shift0, lp, *, batch, seq, rows):
    nt = seq // rows
    bsel = (lambda b: b) if conv0.shape[0] == batch else (lambda b: 0)
    full = lambda shape: pl.BlockSpec(shape, lambda b, t: (0,) * len(shape))
    return pl.pallas_call(
        functools.partial(_inproj_prep_body, rows=rows),
        grid=(batch, nt),
        in_specs=[
            pl.BlockSpec((rows, D_MODEL), lambda b, t: (b * nt + t, 0)),
            full((1, D_MODEL)), full((D_MODEL, P_COLS)),
            pl.BlockSpec((1, SUBLANES, DN_QKV), lambda b, t: (bsel(b), 0, 0)),
            pl.BlockSpec((1, SUBLANES, RW_COLS), lambda b, t: (bsel(b), 0, 0)),
            full((SUBLANES, DN_QKV)), full((1, RW_COLS)),
        ],
        out_specs=[
            pl.BlockSpec((rows, P_COLS), lambda b, t: (b * nt + t, 0)),
            pl.BlockSpec((1, SUBLANES, DN_QKV), lambda b, t: (b, 0, 0)),
            pl.BlockSpec((1, SUBLANES, RW_COLS), lambda b, t: (b, 0, 0)),
        ],
        out_shape=[jax.ShapeDtypeStruct((batch * seq, P_COLS), F32),
                   jax.ShapeDtypeStruct((batch, SUBLANES, DN_QKV), F32),
                   jax.ShapeDtypeStruct((batch, SUBLANES, RW_COLS), F32)],
        scratch_shapes=[pltpu.VMEM((SUBLANES, DN_QKV), F32), pltpu.VMEM((SUBLANES, RW_COLS), F32)],
        compiler_params=pltpu.CompilerParams(
            dimension_semantics=("parallel", "arbitrary"), vmem_limit_bytes=VMEM_LIMIT),
        name="inproj_prep",
    )(x, g, w, conv0, shift0, lp["conv_w"], lp["mu"])


FF_CHUNK = 256


def _post_body(x_ref, m_ref, wo_ref, g1_ref, g2_ref, wg_ref, wu_ref, wd_ref, g3_ref, o_ref):
    mixed = _bdot(m_ref[...], wo_ref[...])
    x1 = x_ref[...] + _rms(mixed, g1_ref[...])
    h = _rms(x1, g2_ref[...]).astype(BF16)
    acc = jnp.zeros(x1.shape, F32)
    for c in range(D_FF // FF_CHUNK):
        sl = slice(c * FF_CHUNK, (c + 1) * FF_CHUNK)
        gate = jnp.dot(h, wg_ref[:, sl], preferred_element_type=F32)
        up = jnp.dot(h, wu_ref[:, sl], preferred_element_type=F32)
        act = gate * _sigmoid(gate) * up
        acc = acc + _bdot(act, wd_ref[sl, :])
    o_ref[...] = x1 + _rms(acc, g3_ref[...])


def _post(x, mixed, wo, g1, g2, wg, wu, wd, g3):
    rows = x.shape[0]
    tm = min(rows, 512)
    row_spec = pl.BlockSpec((tm, D_MODEL), lambda i: (i, 0))
    vec_spec = pl.BlockSpec((1, D_MODEL), lambda i: (0, 0))

    def resident(shape):
        return pl.BlockSpec(shape, lambda i: (0, 0), pipeline_mode=pl.Buffered(1))

    return pl.pallas_call(
        _post_body,
        grid=(rows // tm,),
        in_specs=[row_spec, row_spec, resident((D_MODEL, D_MODEL)), vec_spec, vec_spec,
                  resident((D_MODEL, D_FF)), resident((D_MODEL, D_FF)), resident((D_FF, D_MODEL)),
                  vec_spec],
        out_specs=row_spec,
        out_shape=jax.ShapeDtypeStruct((rows, D_MODEL), F32),
        compiler_params=pltpu.CompilerParams(
            dimension_semantics=("parallel",), vmem_limit_bytes=VMEM_LIMIT),
        name="post",
    )(x, mixed, wo, g1, g2, wg, wu, wd, g3)


def _inv_unit_lower_many(mats, eye, chunk):
    ns = [-a for a in mats]
    ps = [eye + n for n in ns]
    for _ in range(max(int(math.ceil(math.log2(chunk))) - 1, 0)):
        ns = [_mm(n, n) for n in ns]
        ps = [p + _mm(p, n) for p, n in zip(ps, ns)]
    return ps


def _stack(*parts):
    return jnp.concatenate(parts, axis=0)


def _pair_blockdiag(s0, s1):
    zero = jnp.zeros_like(s0)
    return jnp.concatenate([jnp.concatenate([s0, zero], axis=1),
                            jnp.concatenate([zero, s1], axis=1)], axis=0)


def _mixer_body(proj_ref, *refs, rows, group, chunk, n_pad, carry):
    if carry:
        (dn0_ref, rw0_ref, convw_ref, gdnp_ref, dnnorm_ref, mu_ref, rwvec_ref, wup_ref, aup_ref, gup_ref,
         mixed_ref, dn_out_ref, rw_out_ref, *scratch) = refs
    else:
        (hconv_ref, hshift_ref, dn0_ref, rw0_ref, _, _, convw_ref, gdnp_ref, dnnorm_ref, mu_ref, rwvec_ref,
         wup_ref, aup_ref, gup_ref, mixed_ref, dn_out_ref, rw_out_ref, *scratch) = refs
    R, GS, C = rows, group, chunk
    NG = R // GS
    NS = GS // C
    S2 = 2 * GS
    H = SUBLANES
    shift = int(math.log2(C))
    t = pl.program_id(1)

    D = RW_HEAD_DIM
    if carry:
        dn_s, rw_s = scratch

        @pl.when(t == 0)
        def _():
            dn_s[...] = dn0_ref[0]
            for p in range(RW_HEADS // 2):
                rw_s[p] = _pair_blockdiag(rw0_ref[0, 2 * p], rw0_ref[0, 2 * p + 1])

    ri = lax.broadcasted_iota(jnp.int32, (S2, S2), 0)
    ci = lax.broadcasted_iota(jnp.int32, (S2, S2), 1)
    same = (ri >> shift) == (ci >> shift)
    strict = same & (ci < ri)
    incl = same & (ci <= ri)
    eye2 = (ri == ci).astype(F32)
    r1 = lax.broadcasted_iota(jnp.int32, (GS, GS), 0)
    c1 = lax.broadcasted_iota(jnp.int32, (GS, GS), 1)
    lt_blk = (((r1 >> shift) == (c1 >> shift)) & (c1 <= r1)).astype(F32)
    last_sel = (c1 == (((r1 >> shift) << shift) + (C - 1))).astype(F32)
    lane = lax.broadcasted_iota(jnp.int32, (1, LANES), 1)
    m0 = (lane < RW_HEAD_DIM).astype(F32)
    m1 = 1.0 - m0
    bi = lax.broadcasted_iota(jnp.int32, (LANES, LANES), 0)
    bj = lax.broadcasted_iota(jnp.int32, (LANES, LANES), 1)
    bd = ((bi >= RW_HEAD_DIM) == (bj >= RW_HEAD_DIM)).astype(F32)
    if n_pad:
        rowm = ((lax.broadcasted_iota(jnp.int32, (R, 1), 0) & (C - 1)) >= n_pad).astype(F32)
    else:
        rowm = None
    grp = [slice(g * GS, (g + 1) * GS) for g in range(NG)]

    def block_last(x):
        if NS == 1:
            return jnp.broadcast_to(x[GS - 1:GS, :], x.shape)
        return _mm_sel(last_sel, x)

    hl = lambda base, h: slice(base + h * LANES, base + (h + 1) * LANES)
    if carry:
        qn = [proj_ref[:, hl(0, h)] for h in range(DN_HEADS)]
        kn = [proj_ref[:, hl(DN_WIDTH, h)] for h in range(DN_HEADS)]
        vv = [proj_ref[:, hl(2 * DN_WIDTH, h)] for h in range(DN_HEADS)]
        xs = proj_ref[:, C_RW:C_RW + RW_COLS]
        z = proj_ref[:, C_Z:C_Z + DN_WIDTH]
        bg = proj_ref[:, C_BG:C_BG + LANES]
        put = lambda lanes, val: mixed_ref.__setitem__((slice(None), lanes), val)
    else:
        n_real = C - n_pad
        pr = lax.broadcasted_iota(jnp.int32, (R, R * n_real // C), 0)
        rr = lax.broadcasted_iota(jnp.int32, (R, R * n_real // C), 1)
        same_seq = (pr >> shift) == (rr >> int(math.log2(n_real)))
        e_tok = (same_seq & ((pr & (C - 1)) - n_pad == (rr & (n_real - 1)))).astype(F32)
        pc = lax.broadcasted_iota(jnp.int32, (R * n_real // C, R), 1)
        rc = lax.broadcasted_iota(jnp.int32, (R * n_real // C, R), 0)
        e_back = (((pc >> shift) == (rc >> int(math.log2(n_real))))
                  & ((pc & (C - 1)) - n_pad == (rc & (n_real - 1)))).astype(F32)
        ps = lax.broadcasted_iota(jnp.int32, (R, R // C), 0)
        sq = lax.broadcasted_iota(jnp.int32, (R, R // C), 1)
        e_row = lambda k: (((ps >> shift) == sq) & ((ps & (C - 1)) == n_pad - k)).astype(F32)
        tile_qkv = _mm_sel(e_tok, proj_ref[:, C_QKV:C_QKV + DN_QKV])
        for i in range(CONV_W - 1):
            tile_qkv = tile_qkv + _mm_sel(e_row(CONV_W - 1 - i), hconv_ref[i])
        tile_rw = (_mm_sel(e_tok, proj_ref[:, C_RW:C_RW + RW_COLS]) + _mm_sel(e_row(1), hshift_ref[...]))
        zbg = _mm_sel(e_tok, proj_ref[:, C_Z:C_RW])
        z = zbg[:, :DN_WIDTH]
        bg = zbg[:, DN_WIDTH:]
        parts, xs, _, _ = _row_prep(jnp.concatenate([tile_qkv, zbg, tile_rw], axis=1),
                                    jnp.zeros((H, DN_QKV), F32), jnp.zeros((H, RW_COLS), F32),
                                    convw_ref[...], mu_ref[...], R)
        qn, kn, vv = parts[:DN_HEADS], parts[DN_HEADS:2 * DN_HEADS], parts[2 * DN_HEADS:]
        put = lambda lanes, val: mixed_ref.__setitem__((slice(None), lanes), _mm_sel(e_back, val))

    gp = gdnp_ref[...]
    beta_all = _sigmoid(bg)
    glog_all = -jnp.exp(gp[1:2, :]) * _softplus(bg + gp[0:1, :])
    if rowm is not None:
        beta_all = beta_all * rowm
        glog_all = glog_all * rowm
    gcum = [_mm_sel(lt_blk, glog_all[rs]) for rs in grp]
    glast = [block_last(gc) for gc in gcum]

    r = xs[:, 0:RW_WIDTH]
    kr = xs[:, RW_WIDTH:2 * RW_WIDTH]
    vr = xs[:, 2 * RW_WIDTH:3 * RW_WIDTH]
    wa = xs[:, 3 * RW_WIDTH:3 * RW_WIDTH + LANES]
    gd = xs[:, 3 * RW_WIDTH + LANES:RW_COLS]
    rv = rwvec_ref[...]
    w0, a0, k_k, k_a, r_k, ln_w, ln_b = (rv[i:i + 1, :] for i in range(7))
    w_log = -_softplus(-(w0 + _mm(jnp.tanh(wa), wup_ref[...]))) - 0.5
    lw = -jnp.exp(w_log)
    a = _sigmoid(a0 + _mm(wa, aup_ref[...]))
    g = _mm(_sigmoid(gd), gup_ref[...])
    kk_raw = kr * k_k
    kr2 = kr * (1.0 + (a - 1.0) * k_a)
    bonus_in = r * kr2 * r_k
    if rowm is not None:
        lw = lw * rowm
        kr2 = kr2 * rowm
    cl_g = [_mm_sel(lt_blk, lw[rs]) for rs in grp]
    cl = cl_g[0] if NG == 1 else jnp.concatenate(cl_g, axis=0)
    p_last = [jnp.exp(block_last(c)) for c in cl_g]
    e_cl = jnp.exp(cl)
    e_prev = jnp.exp(cl - lw)
    e_inv = jnp.exp(-cl)
    r_t = r * e_cl
    k_t = kr2 * e_inv
    pl_ = lambda p: slice(p * LANES, (p + 1) * LANES)
    kk = []
    for p in range(RW_HEADS // 2):
        kp = kk_raw[:, pl_(p)]
        kp = kp * lax.rsqrt(_mm(kp * kp, bd) + L2_EPS)
        if rowm is not None:
            kp = kp * rowm
        kk.append(kp)

    gd_keys = [(gi, p) for gi in range(NG) for p in range(DN_HEADS // 2)]

    def gstack(arrs, gi, p):
        return _stack(arrs[2 * p][grp[gi]], arrs[2 * p + 1][grp[gi]])

    def gcol(mat, col, p):
        return _stack(mat[:, col + 2 * p:col + 2 * p + 1], mat[:, col + 2 * p + 1:col + 2 * p + 2])

    q2 = [gstack(qn, gi, p) for gi, p in gd_keys]
    k2 = [gstack(kn, gi, p) for gi, p in gd_keys]
    v2 = [gstack(vv, gi, p) for gi, p in gd_keys]
    beta2 = [gcol(beta_all[grp[gi]], 0, p) for gi, p in gd_keys]
    gc2 = [gcol(gcum[gi], DN_HEADS, p) for gi, p in gd_keys]
    gl2 = [gcol(glast[gi], DN_HEADS, p) for gi, p in gd_keys]
    gct = [jnp.transpose(gc) for gc in gcum]
    rb = [jnp.concatenate([gct[gi][DN_HEADS + 2 * p:DN_HEADS + 2 * p + 1, :],
                           gct[gi][DN_HEADS + 2 * p + 1:DN_HEADS + 2 * p + 2, :]], axis=1)
          for gi, p in gd_keys]
    dec = [jnp.exp(jnp.where(incl, c - b, 0.0)) for c, b in zip(gc2, rb)]
    kb = [k * b for k, b in zip(k2, beta2)]
    s_kk = [_mm_nt(x, k) for x, k in zip(kb, k2)]
    s_qk = [_mm_nt(x, k) for x, k in zip(q2, k2)]
    a_mat = [jnp.where(strict, s * d, 0.0) for s, d in zip(s_kk, dec)]
    qk = [jnp.where(incl, s * d, 0.0) for s, d in zip(s_qk, dec)]
    t_dn = _inv_unit_lower_many(a_mat, eye2, C)
    eg = [jnp.exp(c) for c in gc2]
    uw = [_mm(tm, jnp.concatenate([v * b, x * e], axis=1))
          for tm, v, b, x, e in zip(t_dn, v2, beta2, kb, eg)]
    qo = [_mm(m, x) for m, x in zip(qk, uw)]
    u_dn = [x[:, :LANES] for x in uw]
    w_dn = [x[:, LANES:] for x in uw]
    o2_dn = [x[:, :LANES] for x in qo]
    qg2 = [q * e - x[:, LANES:] for q, e, x in zip(q2, eg, qo)]
    k_dec = [k * jnp.exp(l - c) for k, l, c in zip(k2, gl2, gc2)]
    e_gl = [jnp.exp(l) for l in gl2]

    rw_keys = [(gi, p) for gi in range(NG) for p in range(RW_HEADS // 2)]
    kk_t = [kk[p][grp[gi]] * e_prev[grp[gi], pl_(p)] for gi, p in rw_keys]
    a_t = [kk[p][grp[gi]] * a[grp[gi], pl_(p)] * e_inv[grp[gi], pl_(p)] for gi, p in rw_keys]
    rt_p = [r_t[grp[gi], pl_(p)] for gi, p in rw_keys]
    kt_p = [k_t[grp[gi], pl_(p)] for gi, p in rw_keys]
    v_p = [vr[grp[gi], pl_(p)] for gi, p in rw_keys]
    l_kk = [_stack(x * m0, x * m1) for x in kk_t]
    l_r = [_stack(x * m0, x * m1) for x in rt_p]
    vs = [_stack(x * m0, x * m1) for x in v_p]
    if S2 % LANES == 0:
        sc = [_mm_nt(_stack(lk, lr), _stack(x, x, y, y))
              for lk, lr, x, y in zip(l_kk, l_r, a_t, kt_p)]
        m_ka = [jnp.where(strict, s[:S2, :S2], 0.0) for s in sc]
        m_kv = [jnp.where(strict, s[:S2, S2:], 0.0) for s in sc]
        a_ra = [jnp.where(incl, s[S2:, :S2], 0.0) for s in sc]
        a_rk = [jnp.where(incl, s[S2:, S2:], 0.0) for s in sc]
    else:
        ra2 = [_stack(x, x) for x in a_t]
        rk2 = [_stack(x, x) for x in kt_p]
        m_ka = [jnp.where(strict, _mm_nt(x, y), 0.0) for x, y in zip(l_kk, ra2)]
        m_kv = [jnp.where(strict, _mm_nt(x, y), 0.0) for x, y in zip(l_kk, rk2)]
        a_ra = [jnp.where(incl, _mm_nt(x, y), 0.0) for x, y in zip(l_r, ra2)]
        a_rk = [jnp.where(incl, _mm_nt(x, y), 0.0) for x, y in zip(l_r, rk2)]
    t_rw = _inv_unit_lower_many(m_ka, eye2, C)
    mv = [_mm(m, x) for m, x in zip(m_kv, vs)]
    tlw = [_mm(tm, jnp.concatenate([lk, x], axis=1)) for tm, lk, x in zip(t_rw, l_kk, mv)]
    aa = [_mm(m, x) for m, x in zip(a_ra, tlw)]
    av = [_mm(m, x) for m, x in zip(a_rk, vs)]
    tl = [x[:, :LANES] for x in tlw]
    w2 = [x[:, LANES:] for x in tlw]
    ql = [lr - x[:, :LANES] for lr, x in zip(l_r, aa)]
    y2 = [x - y[:, LANES:] for x, y in zip(av, aa)]

    o_rows = [[None] * NG for _ in range(DN_HEADS)]
    y_rows = [[None] * NG for _ in range(RW_HEADS // 2)]
    if carry:
        s_dn = [dn_s[h] for h in range(DN_HEADS)]
        s_rw = [rw_s[p] for p in range(RW_HEADS // 2)]
        for gi in range(NG):
            hsl = lambda e: slice(e * GS, (e + 1) * GS)
            dn_i = [(gd_keys.index((gi, h // 2)), h % 2) for h in range(DN_HEADS)]
            rw_i = [rw_keys.index((gi, p)) for p in range(RW_HEADS // 2)]
            ss_dn = [_mm(_stack(w_dn[i][hsl(e)], qg2[i][hsl(e)]), s_dn[h])
                     for h, (i, e) in enumerate(dn_i)]
            ss_rw = [_mm_nt(_stack(tl[i], ql[i]), s_rw[p]) for p, i in enumerate(rw_i)]
            v_new = [u_dn[i][hsl(e)] - ss[:GS] for ss, (i, e) in zip(ss_dn, dn_i)]
            for h, (ss, (i, e)) in enumerate(zip(ss_dn, dn_i)):
                o_rows[h][gi] = ss[GS:] + o2_dn[i][hsl(e)]
            us = [ss[:S2] + w2[i] for ss, i in zip(ss_rw, rw_i)]
            for p, (ss, i) in enumerate(zip(ss_rw, rw_i)):
                ys = ss[S2:] + y2[i]
                y_rows[p][gi] = ys[:GS] + ys[GS:]
            u1 = [x[:GS] + x[GS:] for x in us]
            s_dn = [s * e_gl[i][e * GS:e * GS + 1, :] + _mm_tn(k_dec[i][hsl(e)], vn)
                    for s, vn, (i, e) in zip(s_dn, v_new, dn_i)]
            s_rw = [(s + _mm_tn(_stack(v_p[i], -x), _stack(kt_p[i], a_t[i]))) * bd
                    * p_last[gi][0:1, pl_(p)]
                    for p, (s, x, i) in enumerate(zip(s_rw, u1, rw_i))]
        for h in range(DN_HEADS):
            dn_s[h] = s_dn[h]
        for p in range(RW_HEADS // 2):
            rw_s[p] = s_rw[p]

        @pl.when(t == pl.num_programs(1) - 1)
        def _():
            dn_out_ref[0] = dn_s[...]
            for p in range(RW_HEADS // 2):
                rw_out_ref[0, 2 * p] = rw_s[p][:D, :D]
                rw_out_ref[0, 2 * p + 1] = rw_s[p][D:, D:]
    else:
        for gi in range(NG):
            seqs = range(NS)
            rsl = lambda e, j: slice(e * GS + j * C, e * GS + (j + 1) * C)
            dn_items = [(j, h, gd_keys.index((gi, h // 2)), h % 2) for j in seqs for h in range(DN_HEADS)]
            rw_items = [(j, p, rw_keys.index((gi, p))) for j in seqs for p in range(RW_HEADS // 2)]
            s_dn = [dn0_ref[gi * NS + j, h] for j, h, _, _ in dn_items]
            s_rw = [_pair_blockdiag(rw0_ref[gi * NS + j, 2 * p], rw0_ref[gi * NS + j, 2 * p + 1])
                    for j, p, _ in rw_items]
            ss_dn = [_mm(_stack(w_dn[i][rsl(e, j)], qg2[i][rsl(e, j)]), s)
                     for s, (j, h, i, e) in zip(s_dn, dn_items)]
            ss_rw = [_mm_nt(_stack(tl[i][rsl(0, j)], tl[i][rsl(1, j)], ql[i][rsl(0, j)], ql[i][rsl(1, j)]), s)
                     for s, (j, p, i) in zip(s_rw, rw_items)]
            v_new = [u_dn[i][rsl(e, j)] - ss[:C] for ss, (j, h, i, e) in zip(ss_dn, dn_items)]
            o_seq = [ss[C:] + o2_dn[i][rsl(e, j)] for ss, (j, h, i, e) in zip(ss_dn, dn_items)]
            us = [ss[:2 * C] + _stack(w2[i][rsl(0, j)], w2[i][rsl(1, j)]) for ss, (j, p, i) in zip(ss_rw, rw_items)]
            ysq = [ss[2 * C:] + _stack(y2[i][rsl(0, j)], y2[i][rsl(1, j)]) for ss, (j, p, i) in zip(ss_rw, rw_items)]
            u1 = [x[:C] + x[C:] for x in us]
            y_seq = [x[:C] + x[C:] for x in ysq]
            for s, vn, (j, h, i, e) in zip(s_dn, v_new, dn_items):
                row = e * GS + j * C
                dn_out_ref[gi * NS + j, h] = (s * e_gl[i][row:row + 1, :]
                                             + _mm_tn(k_dec[i][rsl(e, j)], vn))
            for s, x, (j, p, i) in zip(s_rw, u1, rw_items):
                sq = slice(j * C, (j + 1) * C)
                s_new = ((s + _mm_tn(_stack(v_p[i][sq], -x), _stack(kt_p[i][sq], a_t[i][sq])))
                         * bd * p_last[gi][j * C:j * C + 1, pl_(p)])
                rw_out_ref[gi * NS + j, 2 * p] = s_new[:D, :D]
                rw_out_ref[gi * NS + j, 2 * p + 1] = s_new[D:, D:]
            for h in range(DN_HEADS):
                parts = [o for o, (j, hh, _, _) in zip(o_seq, dn_items) if hh == h]
                o_rows[h][gi] = parts[0] if NS == 1 else jnp.concatenate(parts, axis=0)
            for p in range(RW_HEADS // 2):
                parts = [y for y, (j, pp, _) in zip(y_seq, rw_items) if pp == p]
                y_rows[p][gi] = parts[0] if NS == 1 else jnp.concatenate(parts, axis=0)

    cat = lambda parts: parts[0] if len(parts) == 1 else jnp.concatenate(parts, axis=0)
    dn_w = dnnorm_ref[...]
    for h in range(DN_HEADS):
        zh = z[:, h * LANES:(h + 1) * LANES]
        put(slice(h * LANES, (h + 1) * LANES), _rms(cat(o_rows[h]), dn_w) * (zh * _sigmoid(zh)))
    for p in range(RW_HEADS // 2):
        sl = pl_(p)
        y = cat(y_rows[p])
        mu_g = _mm(y, bd) * (1.0 / RW_HEAD_DIM)
        yc = y - mu_g
        var = _mm(yc * yc, bd) * (1.0 / RW_HEAD_DIM)
        yn = yc * lax.rsqrt(var + GN_EPS) * ln_w[:, sl] + ln_b[:, sl]
        bonus = _mm(bonus_in[:, sl], bd) * vr[:, sl]
        put(slice(DN_WIDTH + p * LANES, DN_WIDTH + (p + 1) * LANES), (yn + bonus) * g[:, sl])


def _mixer(proj, dn0, rw0, lp, *, batch, seq, rows, group, chunk, n_pad, carry, layer=None, hist=(),
           acc=()):
    full = lambda shape: pl.BlockSpec(shape, lambda b, t: (0,) * len(shape))
    dn_blk = (DN_HEADS, DN_HEAD_DIM, DN_HEAD_DIM)
    rw_blk = (RW_HEADS, RW_HEAD_DIM, RW_HEAD_DIM)
    if carry:
        nt = seq // rows
        grid = (batch, nt)
        bsel = (lambda b: b) if dn0.shape[0] == batch else (lambda b: 0)
        row_map = lambda b, t: (b * nt + t, 0)
        st_in = [pl.BlockSpec((1,) + blk, lambda b, t: (bsel(b), 0, 0, 0)) for blk in (dn_blk, rw_blk)]
        st_out = [pl.BlockSpec((1,) + blk, lambda b, t: (b, 0, 0, 0)) for blk in (dn_blk, rw_blk)]
        scratch = [pltpu.VMEM((DN_HEADS, LANES, LANES), F32), pltpu.VMEM((RW_HEADS // 2, LANES, LANES), F32)]
        io_rows = rows
    else:
        nseq = rows // chunk
        grid = (batch // nseq, 1)
        row_map = lambda b, t: (b, 0)
        io_rows = nseq * (chunk - n_pad)
        st_in = [pl.BlockSpec((None, nseq) + blk, lambda b, t: (layer, b, 0, 0, 0)) for blk in (dn_blk, rw_blk)]
        st_in += [pl.BlockSpec(memory_space=pl.ANY)] * len(acc)
        st_out = [pl.BlockSpec((None, nseq) + blk, lambda b, t: (layer, b, 0, 0, 0)) for blk in (dn_blk, rw_blk)]
        scratch = []
    state_shapes = ([jax.ShapeDtypeStruct(a.shape, F32) for a in acc] if acc else
                    [jax.ShapeDtypeStruct((batch,) + blk, F32) for blk in (dn_blk, rw_blk)])
    first_acc = 1 + len(hist) + 2
    return pl.pallas_call(
        functools.partial(_mixer_body, rows=rows, group=group, chunk=chunk, n_pad=n_pad, carry=carry),
        grid=grid,
        in_specs=[
            pl.BlockSpec((io_rows, P_COLS), row_map),
            *[pl.BlockSpec(h.shape[:-2] + (rows // chunk, h.shape[-1]), lambda b, t, n=h.ndim: (0,) * (n - 2) + (b, 0))
              for h in hist],
            *st_in,
            full((SUBLANES, DN_QKV)), full((SUBLANES, LANES)), full((1, LANES)),
            full((1, RW_COLS)), full((SUBLANES, RW_WIDTH)),
            full((LANES, RW_WIDTH)), full((LANES, RW_WIDTH)), full((G_LORA, RW_WIDTH)),
        ],
        out_specs=[pl.BlockSpec((io_rows, D_MODEL), row_map), *st_out],
        out_shape=[jax.ShapeDtypeStruct((proj.shape[0], D_MODEL), F32), *state_shapes],
        input_output_aliases={first_acc + i: 1 + i for i in range(len(acc))},
        scratch_shapes=scratch,
        compiler_params=pltpu.CompilerParams(
            dimension_semantics=("parallel", "arbitrary"), vmem_limit_bytes=VMEM_LIMIT),
        name="mixer",
    )(proj, *hist, dn0, rw0, *acc, lp["conv_w"], lp["gdnp"], lp["dn_norm"], lp["mu"],
      lp["rwvec"], lp["w_up"], lp["a_up"], lp["g_up"])


def _prep_layer_params(l, w_in, dn_conv_w, dn_a_log, dn_dt_bias, dn_norm_w, rw_mu, rw_w0, rw_w_up,
                       rw_a0, rw_a_up, rw_g_up, rw_k_k, rw_k_a, rw_r_k, rw_ln_w, rw_ln_b):
    off = DN_QKV + DN_WIDTH
    wi = w_in[l]
    w_in_p = jnp.concatenate(
        [wi[:, :off + 2 * DN_HEADS], jnp.zeros((D_MODEL, LANES - 2 * DN_HEADS), F32),
         wi[:, off + 2 * DN_HEADS:]], axis=1).astype(BF16)
    gdnp = jnp.zeros((SUBLANES, LANES), F32)
    gdnp = gdnp.at[0, DN_HEADS:2 * DN_HEADS].set(dn_dt_bias[l])
    gdnp = gdnp.at[1, DN_HEADS:2 * DN_HEADS].set(dn_a_log[l])
    rwvec = jnp.stack([rw_w0[l], rw_a0[l], rw_k_k[l], rw_k_a[l], rw_r_k[l].reshape(RW_WIDTH),
                       rw_ln_w[l], rw_ln_b[l], jnp.zeros((RW_WIDTH,), F32)])
    zeros_lora = jnp.zeros((W_LORA, RW_WIDTH), F32)
    return {
        "w_in": w_in_p,
        "conv_w": jnp.concatenate([dn_conv_w[l], jnp.zeros((SUBLANES - CONV_W, DN_QKV), F32)], axis=0),
        "gdnp": gdnp,
        "dn_norm": dn_norm_w[l].reshape(1, LANES),
        "mu": rw_mu[l].reshape(1, RW_COLS),
        "rwvec": rwvec,
        "w_up": jnp.concatenate([rw_w_up[l], zeros_lora], axis=0),
        "a_up": jnp.concatenate([zeros_lora, rw_a_up[l]], axis=0),
        "g_up": rw_g_up[l],
    }


def _post_layer(x, mixed, ffn, l):
    return _post(x, mixed, ffn["w_out"][l], ffn["g_post_mix"][l], ffn["g_pre_ffn"][l],
                 ffn["w_gate"][l], ffn["w_up"][l], ffn["w_down"][l], ffn["g_post_ffn"][l])


def _trunk_carry(x, states, layers, ffn, *, batch, seq, rows, group):
    new_states = []
    for l in range(DEPTH):
        conv_s, dn_s, shift_s, rw_s = states[l]
        prep, conv_t, shift_t = _inproj_prep(x, ffn["g_pre_mix"][l], layers[l]["w_in"], conv_s, shift_s,
                                             layers[l], batch=batch, seq=seq, rows=rows)
        mixed, dn_new, rw_new = _mixer(prep, dn_s, rw_s, layers[l], batch=batch, seq=seq, rows=rows,
                                       group=group, chunk=group, n_pad=0, carry=True)
        x = _post_layer(x, mixed, ffn, l)
        new_states.append((conv_t, dn_new, shift_t, rw_new))
    return x, new_states


def _trunk_sample(x, state_dn_conv, state_dn, state_rw_shift, state_rw, layers, ffn, *, batch, seq_real):
    seq = SUBLANES
    n_pad = seq - seq_real
    assert n_pad >= CONV_W - 1, "the masked front rows must hold the whole conv history"
    conv_new, shift_new = [], []
    dn_acc = jnp.zeros(state_dn.shape, F32)
    rw_acc = jnp.zeros(state_rw.shape, F32)
    for l in range(DEPTH):
        proj = _inproj(x, ffn["g_pre_mix"][l], layers[l]["w_in"])
        hist = (jnp.transpose(state_dn_conv[l], (1, 0, 2)), state_rw_shift[l])
        mixed, dn_acc, rw_acc = _mixer(proj, state_dn, state_rw, layers[l], batch=batch, seq=seq,
                                       rows=SAMPLE_ROWS, group=GROUP, chunk=seq, n_pad=n_pad, carry=False,
                                       layer=l, hist=hist, acc=(dn_acc, rw_acc))
        x = _post_layer(x, mixed, ffn, l)
        conv_new.append(proj[:, C_QKV:C_QKV + DN_QKV].reshape(batch, seq_real, DN_QKV)[:, 1 - CONV_W:])
        shift_new.append(proj[:, C_RW:].reshape(batch, seq_real, RW_COLS)[:, -1])
    return x, (jnp.stack(conv_new), dn_acc, jnp.stack(shift_new), rw_acc)


def kernel(x_prompt, x_sample, state_dn_conv, state_dn, state_rw_shift, state_rw, meta, g_pre_mix, g_post_mix, g_pre_ffn, g_post_ffn, w_in, dn_conv_w, dn_a_log, dn_dt_bias, dn_norm_w, rw_mu, rw_w0, rw_w_up, rw_a0, rw_a_up, rw_g_up, rw_k_k, rw_k_a, rw_r_k, rw_ln_w, rw_ln_b, w_out, w_gate, w_up, w_down):
    batch, seq, _ = x_prompt.shape
    dec_batch, dec_seq, _ = x_sample.shape
    layers = [_prep_layer_params(l, w_in, dn_conv_w, dn_a_log, dn_dt_bias, dn_norm_w, rw_mu, rw_w0,
                                 rw_w_up, rw_a0, rw_a_up, rw_g_up, rw_k_k, rw_k_a, rw_r_k, rw_ln_w,
                                 rw_ln_b) for l in range(DEPTH)]
    vec = lambda g: g.reshape(DEPTH, 1, D_MODEL)
    ffn = {"g_pre_mix": vec(g_pre_mix), "g_post_mix": vec(g_post_mix), "g_pre_ffn": vec(g_pre_ffn),
           "g_post_ffn": vec(g_post_ffn), "w_out": w_out.astype(BF16), "w_gate": w_gate.astype(BF16),
           "w_up": w_up.astype(BF16), "w_down": w_down.astype(BF16)}

    zero_states = [(jnp.zeros((1, SUBLANES, DN_QKV), F32), jnp.zeros((1, DN_HEADS, DN_HEAD_DIM, DN_HEAD_DIM), F32),
                    jnp.zeros((1, SUBLANES, RW_COLS), F32),
                    jnp.zeros((1, RW_HEADS, RW_HEAD_DIM, RW_HEAD_DIM), F32))] * DEPTH
    _, meta_states = _trunk_carry(meta.astype(F32), zero_states, layers, ffn,
                                  batch=1, seq=N_META, rows=N_META, group=N_META)

    yp, ps = _trunk_carry(x_prompt.reshape(batch * seq, D_MODEL), meta_states, layers, ffn,
                          batch=batch, seq=seq, rows=PROMPT_ROWS, group=GROUP)
    ps = [(c[:, SUBLANES - (CONV_W - 1):], d, s[:, SUBLANES - 1], r) for c, d, s, r in ps]

    ys, (cs, ds, sh, rs) = _trunk_sample(x_sample.reshape(dec_batch * dec_seq, D_MODEL), state_dn_conv,
                                         state_dn, state_rw_shift, state_rw, layers, ffn,
                                         batch=dec_batch, seq_real=dec_seq)

    cp, dp, sp, rp = (jnp.stack([s[i] for s in ps]) for i in range(4))
    return (yp.reshape(batch, seq, D_MODEL), ys.reshape(dec_batch, dec_seq, D_MODEL),
            cp, dp, sp, rp, cs, ds, sh, rs)
```

```python
import functools
import math

import jax
import jax.numpy as jnp
from jax import lax
from jax.experimental import pallas as pl
from jax.experimental.pallas import tpu as pltpu

F32 = jnp.float32
BF16 = jnp.bfloat16

D_MODEL = 1024
DEPTH = 4
N_META = 16
DN_HEADS = 4
DN_HEAD_DIM = 128
DN_WIDTH = DN_HEADS * DN_HEAD_DIM
DN_QKV = 3 * DN_WIDTH
RW_HEADS = 8
RW_HEAD_DIM = 64
RW_WIDTH = RW_HEADS * RW_HEAD_DIM
W_LORA = 64
A_LORA = 64
G_LORA = 128
RW_COLS = 3 * RW_WIDTH + W_LORA + A_LORA + G_LORA
D_FF = 2816
CONV_W = 4
RMS_EPS = 1e-6
L2_EPS = 1e-6
GN_EPS = 64e-5

LANES = 128
SUBLANES = 8
VMEM_LIMIT = 56 * 1024 * 1024

C_QKV = 0
C_Z = DN_QKV
C_BG = C_Z + DN_WIDTH
C_RW = C_BG + LANES
P_COLS = C_RW + RW_COLS
GROUP = 64
PROMPT_ROWS = 512
SAMPLE_ROWS = 128


def _sigmoid(x):
    return 1.0 / (1.0 + jnp.exp(-x))


def _softplus(x):
    return jnp.maximum(x, 0.0) + jnp.log(1.0 + jnp.exp(-jnp.abs(x)))


def _rms(x, g):
    return x * lax.rsqrt(jnp.mean(x * x, axis=-1, keepdims=True) + RMS_EPS) * g


def _mm(a, b):
    return jnp.dot(a.astype(BF16), b.astype(BF16), preferred_element_type=F32)


def _mm_nt(a, b):
    return lax.dot_general(a.astype(BF16), b.astype(BF16), (((1,), (1,)), ((), ())),
                           preferred_element_type=F32)


def _mm_tn(a, b):
    return lax.dot_general(a.astype(BF16), b.astype(BF16), (((0,), (0,)), ((), ())),
                           preferred_element_type=F32)


def _mm_sel(sel, x):
    sel = sel.astype(BF16)
    x1 = x.astype(BF16)
    x2 = (x - x1.astype(F32)).astype(BF16)
    dot = lambda t: jnp.dot(sel, t, preferred_element_type=F32)
    return dot(x1) + dot(x2)


def _bdot(a, b):
    return jnp.dot(a.astype(BF16), b, preferred_element_type=F32)


def _inproj_body(x_ref, g_ref, w_ref, o_ref):
    h = _rms(x_ref[...], g_ref[...])
    o_ref[...] = _bdot(h, w_ref[...])


def _inproj(x, g, w):
    rows = x.shape[0]
    tm = min(rows, 256)
    return pl.pallas_call(
        _inproj_body,
        grid=(rows // tm,),
        in_specs=[
            pl.BlockSpec((tm, D_MODEL), lambda i: (i, 0)),
            pl.BlockSpec((1, D_MODEL), lambda i: (0, 0)),
            pl.BlockSpec((D_MODEL, P_COLS), lambda i: (0, 0)),
        ],
        out_specs=pl.BlockSpec((tm, P_COLS), lambda i: (i, 0)),
        out_shape=jax.ShapeDtypeStruct((rows, P_COLS), F32),
        compiler_params=pltpu.CompilerParams(
            dimension_semantics=("parallel",), vmem_limit_bytes=VMEM_LIMIT),
        name="inproj",
    )(x, g, w)


def _row_prep(proj, halo_qkv, halo_rw, cw, mu, rows):
    H = SUBLANES
    R = rows
    x_qkv = proj[:, C_QKV:C_QKV + DN_QKV]
    ext = jnp.concatenate([halo_qkv, x_qkv], axis=0)
    back = lambda a, k: pltpu.roll(a, k, 0)[H:H + R, :]
    conv = (back(ext, 3) * cw[0:1, :] + back(ext, 2) * cw[1:2, :] + back(ext, 1) * cw[2:3, :]
            + x_qkv * cw[3:4, :])
    half = 0.5 * conv
    qkv = half + half * jnp.tanh(half)

    def l2n(x):
        return x * lax.rsqrt(jnp.sum(x * x, axis=-1, keepdims=True) + L2_EPS)

    hl = lambda base, h: slice(base + h * LANES, base + (h + 1) * LANES)
    parts = ([l2n(qkv[:, hl(0, h)]) * (DN_HEAD_DIM ** -0.5) for h in range(DN_HEADS)]
             + [l2n(qkv[:, hl(DN_WIDTH, h)]) for h in range(DN_HEADS)]
             + [qkv[:, hl(2 * DN_WIDTH, h)] for h in range(DN_HEADS)])
    cols = proj[:, C_RW:C_RW + RW_COLS]
    prev = back(jnp.concatenate([halo_rw, cols], axis=0), 1)
    xs = cols + (prev - cols) * mu
    return parts, xs, x_qkv[R - H:R, :], cols[R - H:R, :]


def _inproj_prep_body(x_ref, g_ref, w_ref, conv0_ref, shift0_ref, cw_ref, mu_ref,
                      o_ref, ctail_ref, stail_ref, halo_qkv, halo_rw, *, rows):
    @pl.when(pl.program_id(1) == 0)
    def _():
        halo_qkv[...] = conv0_ref[0]
        halo_rw[...] = shift0_ref[0]

    proj = _bdot(_rms(x_ref[...], g_ref[...]), w_ref[...])
    parts, xs, tail_qkv, tail_rw = _row_prep(proj, halo_qkv[...], halo_rw[...], cw_ref[...],
                                             mu_ref[...], rows)
    for idx, part in enumerate(parts):
        o_ref[:, idx * LANES:(idx + 1) * LANES] = part
    o_ref[:, C_Z:C_RW] = proj[:, C_Z:C_RW]
    o_ref[:, C_RW:C_RW + RW_COLS] = xs
    halo_qkv[...] = tail_qkv
    halo_rw[...] = tail_rw
    ctail_ref[0] = tail_qkv
    stail_ref[0] = tail_rw


def _inproj_prep(x, g, w, conv0, shift0, lp, *, batch, seq, rows):
    nt = seq // rows
    bsel = (lambda b: b) if conv0.shape[0] == batch else (lambda b: 0)
    full = lambda shape: pl.BlockSpec(shape, lambda b, t: (0,) * len(shape))
    return pl.pallas_call(
        functools.partial(_inproj_prep_body, rows=rows),
        grid=(batch, nt),
        in_specs=[
            pl.BlockSpec((rows, D_MODEL), lambda b, t: (b * nt + t, 0)),
            full((1, D_MODEL)), full((D_MODEL, P_COLS)),
            pl.BlockSpec((1, SUBLANES, DN_QKV), lambda b, t: (bsel(b), 0, 0)),
            pl.BlockSpec((1, SUBLANES, RW_COLS), lambda b, t: (bsel(b), 0, 0)),
            full((SUBLANES, DN_QKV)), full((1, RW_COLS)),
        ],
        out_specs=[
            pl.BlockSpec((rows, P_COLS), lambda b, t: (b * nt + t, 0)),
            pl.BlockSpec((1, SUBLANES, DN_QKV), lambda b, t: (b, 0, 0)),
            pl.BlockSpec((1, SUBLANES, RW_COLS), lambda b, t: (b, 0, 0)),
        ],
        out_shape=[jax.ShapeDtypeStruct((batch * seq, P_COLS), F32),
                   jax.ShapeDtypeStruct((batch, SUBLANES, DN_QKV), F32),
                   jax.ShapeDtypeStruct((batch, SUBLANES, RW_COLS), F32)],
        scratch_shapes=[pltpu.VMEM((SUBLANES, DN_QKV), F32), pltpu.VMEM((SUBLANES, RW_COLS), F32)],
        compiler_params=pltpu.CompilerParams(
            dimension_semantics=("parallel", "arbitrary"), vmem_limit_bytes=VMEM_LIMIT),
        name="inproj_prep",
    )(x, g, w, conv0, shift0, lp["conv_w"], lp["mu"])


FF_CHUNK = 256


def _post_body(x_ref, m_ref, wo_ref, g1_ref, g2_ref, wg_ref, wu_ref, wd_ref, g3_ref, o_ref):
    mixed = _bdot(m_ref[...], wo_ref[...])
    x1 = x_ref[...] + _rms(mixed, g1_ref[...])
    h = _rms(x1, g2_ref[...]).astype(BF16)
    acc = jnp.zeros(x1.shape, F32)
    for c in range(D_FF // FF_CHUNK):
        sl = slice(c * FF_CHUNK, (c + 1) * FF_CHUNK)
        gate = jnp.dot(h, wg_ref[:, sl], preferred_element_type=F32)
        up = jnp.dot(h, wu_ref[:, sl], preferred_element_type=F32)
        act = gate * _sigmoid(gate) * up
        acc = acc + _bdot(act, wd_ref[sl, :])
    o_ref[...] = x1 + _rms(acc, g3_ref[...])


def _post(x, mixed, wo, g1, g2, wg, wu, wd, g3):
    rows = x.shape[0]
    tm = min(rows, 512)
    row_spec = pl.BlockSpec((tm, D_MODEL), lambda i: (i, 0))
    vec_spec = pl.BlockSpec((1, D_MODEL), lambda i: (0, 0))

    def resident(shape):
        return pl.BlockSpec(shape, lambda i: (0, 0), pipeline_mode=pl.Buffered(1))

    return pl.pallas_call(
        _post_body,
        grid=(rows // tm,),
        in_specs=[row_spec, row_spec, resident((D_MODEL, D_MODEL)), vec_spec, vec_spec,
                  resident((D_MODEL, D_FF)), resident((D_MODEL, D_FF)), resident((D_FF, D_MODEL)),
                  vec_spec],
        out_specs=row_spec,
        out_shape=jax.ShapeDtypeStruct((rows, D_MODEL), F32),
        compiler_params=pltpu.CompilerParams(
            dimension_semantics=("parallel",), vmem_limit_bytes=VMEM_LIMIT),
        name="post",
    )(x, mixed, wo, g1, g2, wg, wu, wd, g3)


def _inv_unit_lower_many(mats, eye, chunk):
    ns = [-a for a in mats]
    ps = [eye + n for n in ns]
    for _ in range(max(int(math.ceil(math.log2(chunk))) - 1, 0)):
        ns = [_mm(n, n) for n in ns]
        ps = [p + _mm(p, n) for p, n in zip(ps, ns)]
    return ps


def _stack(*parts):
    return jnp.concatenate(parts, axis=0)


def _pair_blockdiag(s0, s1):
    zero = jnp.zeros_like(s0)
    return jnp.concatenate([jnp.concatenate([s0, zero], axis=1),
                            jnp.concatenate([zero, s1], axis=1)], axis=0)


def _mixer_body(proj_ref, *refs, rows, group, chunk, n_pad, carry):
    if carry:
        (dn0_ref, rw0_ref, convw_ref, gdnp_ref, dnnorm_ref, mu_ref, rwvec_ref, wup_ref, aup_ref, gup_ref,
         mixed_ref, dn_out_ref, rw_out_ref, *scratch) = refs
    else:
        (hconv_ref, hshift_ref, dn0_ref, rw0_ref, _, _, convw_ref, gdnp_ref, dnnorm_ref, mu_ref, rwvec_ref,
         wup_ref, aup_ref, gup_ref, mixed_ref, dn_out_ref, rw_out_ref, *scratch) = refs
    R, GS, C = rows, group, chunk
    NG = R // GS
    NS = GS // C
    S2 = 2 * GS
    H = SUBLANES
    shift = int(math.log2(C))
    t = pl.program_id(1)

    D = RW_HEAD_DIM
    if carry:
        dn_s, rw_s = scratch

        @pl.when(t == 0)
        def _():
            dn_s[...] = dn0_ref[0]
            for p in range(RW_HEADS // 2):
                rw_s[p] = _pair_blockdiag(rw0_ref[0, 2 * p], rw0_ref[0, 2 * p + 1])

    ri = lax.broadcasted_iota(jnp.int32, (S2, S2), 0)
    ci = lax.broadcasted_iota(jnp.int32, (S2, S2), 1)
    same = (ri >> shift) == (ci >> shift)
    strict = same & (ci < ri)
    incl = same & (ci <= ri)
    eye2 = (ri == ci).astype(F32)
    r1 = lax.broadcasted_iota(jnp.int32, (GS, GS), 0)
    c1 = lax.broadcasted_iota(jnp.int32, (GS, GS), 1)
    lt_blk = (((r1 >> shift) == (c1 >> shift)) & (c1 <= r1)).astype(F32)
    last_sel = (c1 == (((r1 >> shift) << shift) + (C - 1))).astype(F32)
    lane = lax.broadcasted_iota(jnp.int32, (1, LANES), 1)
    m0 = (lane < RW_HEAD_DIM).astype(F32)
    m1 = 1.0 - m0

    def gsum(x):
        if carry:
            return _mm(x, bd)
        s0 = jnp.sum(x * m0, axis=-1, keepdims=True)
        s1 = jnp.sum(x * m1, axis=-1, keepdims=True)
        return jnp.where(lane < RW_HEAD_DIM, s0, s1)
    bi = lax.broadcasted_iota(jnp.int32, (LANES, LANES), 0)
    bj = lax.broadcasted_iota(jnp.int32, (LANES, LANES), 1)
    bd = ((bi >= RW_HEAD_DIM) == (bj >= RW_HEAD_DIM)).astype(F32)
    if n_pad:
        rowm = ((lax.broadcasted_iota(jnp.int32, (R, 1), 0) & (C - 1)) >= n_pad).astype(F32)
    else:
        rowm = None
    grp = [slice(g * GS, (g + 1) * GS) for g in range(NG)]

    def block_last(x):
        if NS == 1:
            return jnp.broadcast_to(x[GS - 1:GS, :], x.shape)
        return _mm_sel(last_sel, x)

    hl = lambda base, h: slice(base + h * LANES, base + (h + 1) * LANES)
    if carry:
        qn = [proj_ref[:, hl(0, h)] for h in range(DN_HEADS)]
        kn = [proj_ref[:, hl(DN_WIDTH, h)] for h in range(DN_HEADS)]
        vv = [proj_ref[:, hl(2 * DN_WIDTH, h)] for h in range(DN_HEADS)]
        xs = proj_ref[:, C_RW:C_RW + RW_COLS]
        z = proj_ref[:, C_Z:C_Z + DN_WIDTH]
        bg = proj_ref[:, C_BG:C_BG + LANES]
        put = lambda lanes, val: mixed_ref.__setitem__((slice(None), lanes), val)
    else:
        n_real = C - n_pad
        pr = lax.broadcasted_iota(jnp.int32, (R, R * n_real // C), 0)
        rr = lax.broadcasted_iota(jnp.int32, (R, R * n_real // C), 1)
        same_seq = (pr >> shift) == (rr >> int(math.log2(n_real)))
        e_tok = (same_seq & ((pr & (C - 1)) - n_pad == (rr & (n_real - 1)))).astype(F32)
        pc = lax.broadcasted_iota(jnp.int32, (R * n_real // C, R), 1)
        rc = lax.broadcasted_iota(jnp.int32, (R * n_real // C, R), 0)
        e_back = (((pc >> shift) == (rc >> int(math.log2(n_real))))
                  & ((pc & (C - 1)) - n_pad == (rc & (n_real - 1)))).astype(F32)
        ps = lax.broadcasted_iota(jnp.int32, (R, R // C), 0)
        sq = lax.broadcasted_iota(jnp.int32, (R, R // C), 1)
        e_row = lambda k: (((ps >> shift) == sq) & ((ps & (C - 1)) == n_pad - k)).astype(F32)
        tile_qkv = _mm_sel(e_tok, proj_ref[:, C_QKV:C_QKV + DN_QKV])
        for i in range(CONV_W - 1):
            tile_qkv = tile_qkv + _mm_sel(e_row(CONV_W - 1 - i), hconv_ref[i])
        tile_rw = (_mm_sel(e_tok, proj_ref[:, C_RW:C_RW + RW_COLS]) + _mm_sel(e_row(1), hshift_ref[...]))
        zbg = _mm_sel(e_tok, proj_ref[:, C_Z:C_RW])
        z = zbg[:, :DN_WIDTH]
        bg = zbg[:, DN_WIDTH:]
        parts, xs, _, _ = _row_prep(jnp.concatenate([tile_qkv, zbg, tile_rw], axis=1),
                                    jnp.zeros((H, DN_QKV), F32), jnp.zeros((H, RW_COLS), F32),
                                    convw_ref[...], mu_ref[...], R)
        qn, kn, vv = parts[:DN_HEADS], parts[DN_HEADS:2 * DN_HEADS], parts[2 * DN_HEADS:]
        put = lambda lanes, val: mixed_ref.__setitem__((slice(None), lanes), _mm_sel(e_back, val))

    gp = gdnp_ref[...]
    beta_all = _sigmoid(bg)
    glog_all = -jnp.exp(gp[1:2, :]) * _softplus(bg + gp[0:1, :])
    if rowm is not None:
        beta_all = beta_all * rowm
        glog_all = glog_all * rowm
    gcum = [_mm_sel(lt_blk, glog_all[rs]) for rs in grp]
    glast = [block_last(gc) for gc in gcum]

    r = xs[:, 0:RW_WIDTH]
    kr = xs[:, RW_WIDTH:2 * RW_WIDTH]
    vr = xs[:, 2 * RW_WIDTH:3 * RW_WIDTH]
    wa = xs[:, 3 * RW_WIDTH:3 * RW_WIDTH + LANES]
    gd = xs[:, 3 * RW_WIDTH + LANES:RW_COLS]
    rv = rwvec_ref[...]
    w0, a0, k_k, k_a, r_k, ln_w, ln_b = (rv[i:i + 1, :] for i in range(7))
    w_log = -_softplus(-(w0 + _mm(jnp.tanh(wa), wup_ref[...]))) - 0.5
    lw = -jnp.exp(w_log)
    a = _sigmoid(a0 + _mm(wa, aup_ref[...]))
    g = _mm(_sigmoid(gd), gup_ref[...])
    kk_raw = kr * k_k
    kr2 = kr * (1.0 + (a - 1.0) * k_a)
    bonus_in = r * kr2 * r_k
    if rowm is not None:
        lw = lw * rowm
        kr2 = kr2 * rowm
    cl_g = [_mm_sel(lt_blk, lw[rs]) for rs in grp]
    cl = cl_g[0] if NG == 1 else jnp.concatenate(cl_g, axis=0)
    p_last = [jnp.exp(block_last(c)) for c in cl_g]
    e_cl = jnp.exp(cl)
    e_prev = jnp.exp(cl - lw)
    e_inv = jnp.exp(-cl)
    r_t = r * e_cl
    k_t = kr2 * e_inv
    pl_ = lambda p: slice(p * LANES, (p + 1) * LANES)
    kk = []
    for p in range(RW_HEADS // 2):
        kp = kk_raw[:, pl_(p)]
        kp = kp * lax.rsqrt(gsum(kp * kp) + L2_EPS)
        if rowm is not None:
            kp = kp * rowm
        kk.append(kp)

    gd_keys = [(gi, p) for gi in range(NG) for p in range(DN_HEADS // 2)]

    def gstack(arrs, gi, p):
        return _stack(arrs[2 * p][grp[gi]], arrs[2 * p + 1][grp[gi]])

    def gcol(mat, col, p):
        return _stack(mat[:, col + 2 * p:col + 2 * p + 1], mat[:, col + 2 * p + 1:col + 2 * p + 2])

    q2 = [gstack(qn, gi, p) for gi, p in gd_keys]
    k2 = [gstack(kn, gi, p) for gi, p in gd_keys]
    v2 = [gstack(vv, gi, p) for gi, p in gd_keys]
    beta2 = [gcol(beta_all[grp[gi]], 0, p) for gi, p in gd_keys]
    gc2 = [gcol(gcum[gi], DN_HEADS, p) for gi, p in gd_keys]
    gl2 = [gcol(glast[gi], DN_HEADS, p) for gi, p in gd_keys]
    gct = [jnp.transpose(gc) for gc in gcum]
    rb = [jnp.concatenate([gct[gi][DN_HEADS + 2 * p:DN_HEADS + 2 * p + 1, :],
                           gct[gi][DN_HEADS + 2 * p + 1:DN_HEADS + 2 * p + 2, :]], axis=1)
          for gi, p in gd_keys]
    dec = [jnp.exp(jnp.where(incl, c - b, 0.0)) for c, b in zip(gc2, rb)]
    kb = [k * b for k, b in zip(k2, beta2)]
    s_kk = [_mm_nt(x, k) for x, k in zip(kb, k2)]
    s_qk = [_mm_nt(x, k) for x, k in zip(q2, k2)]
    a_mat = [jnp.where(strict, s * d, 0.0) for s, d in zip(s_kk, dec)]
    qk = [jnp.where(incl, s * d, 0.0) for s, d in zip(s_qk, dec)]
    t_dn = _inv_unit_lower_many(a_mat, eye2, C)
    eg = [jnp.exp(c) for c in gc2]
    uw = [_mm(tm, jnp.concatenate([v * b, x * e], axis=1))
          for tm, v, b, x, e in zip(t_dn, v2, beta2, kb, eg)]
    qo = [_mm(m, x) for m, x in zip(qk, uw)]
    u_dn = [x[:, :LANES] for x in uw]
    w_dn = [x[:, LANES:] for x in uw]
    o2_dn = [x[:, :LANES] for x in qo]
    qg2 = [q * e - x[:, LANES:] for q, e, x in zip(q2, eg, qo)]
    k_dec = [k * jnp.exp(l - c) for k, l, c in zip(k2, gl2, gc2)]
    e_gl = [jnp.exp(l) for l in gl2]

    rw_keys = [(gi, p) for gi in range(NG) for p in range(RW_HEADS // 2)]
    kk_t = [kk[p][grp[gi]] * e_prev[grp[gi], pl_(p)] for gi, p in rw_keys]
    a_t = [kk[p][grp[gi]] * a[grp[gi], pl_(p)] * e_inv[grp[gi], pl_(p)] for gi, p in rw_keys]
    rt_p = [r_t[grp[gi], pl_(p)] for gi, p in rw_keys]
    kt_p = [k_t[grp[gi], pl_(p)] for gi, p in rw_keys]
    v_p = [vr[grp[gi], pl_(p)] for gi, p in rw_keys]
    l_kk = [_stack(x * m0, x * m1) for x in kk_t]
    l_r = [_stack(x * m0, x * m1) for x in rt_p]
    vs = [_stack(x * m0, x * m1) for x in v_p]
    if S2 % LANES == 0:
        sc = [_mm_nt(_stack(lk, lr), _stack(x, x, y, y))
              for lk, lr, x, y in zip(l_kk, l_r, a_t, kt_p)]
        m_ka = [jnp.where(strict, s[:S2, :S2], 0.0) for s in sc]
        m_kv = [jnp.where(strict, s[:S2, S2:], 0.0) for s in sc]
        a_ra = [jnp.where(incl, s[S2:, :S2], 0.0) for s in sc]
        a_rk = [jnp.where(incl, s[S2:, S2:], 0.0) for s in sc]
    else:
        ra2 = [_stack(x, x) for x in a_t]
        rk2 = [_stack(x, x) for x in kt_p]
        m_ka = [jnp.where(strict, _mm_nt(x, y), 0.0) for x, y in zip(l_kk, ra2)]
        m_kv = [jnp.where(strict, _mm_nt(x, y), 0.0) for x, y in zip(l_kk, rk2)]
        a_ra = [jnp.where(incl, _mm_nt(x, y), 0.0) for x, y in zip(l_r, ra2)]
        a_rk = [jnp.where(incl, _mm_nt(x, y), 0.0) for x, y in zip(l_r, rk2)]
    t_rw = _inv_unit_lower_many(m_ka, eye2, C)
    mv = [_mm(m, x) for m, x in zip(m_kv, vs)]
    tlw = [_mm(tm, jnp.concatenate([lk, x], axis=1)) for tm, lk, x in zip(t_rw, l_kk, mv)]
    aa = [_mm(m, x) for m, x in zip(a_ra, tlw)]
    av = [_mm(m, x) for m, x in zip(a_rk, vs)]
    tl = [x[:, :LANES] for x in tlw]
    w2 = [x[:, LANES:] for x in tlw]
    ql = [lr - x[:, :LANES] for lr, x in zip(l_r, aa)]
    y2 = [x - y[:, LANES:] for x, y in zip(av, aa)]

    o_rows = [[None] * NG for _ in range(DN_HEADS)]
    y_rows = [[None] * NG for _ in range(RW_HEADS // 2)]
    if carry:
        s_dn = [dn_s[h] for h in range(DN_HEADS)]
        s_rw = [rw_s[p] for p in range(RW_HEADS // 2)]
        for gi in range(NG):
            hsl = lambda e: slice(e * GS, (e + 1) * GS)
            dn_i = [(gd_keys.index((gi, h // 2)), h % 2) for h in range(DN_HEADS)]
            rw_i = [rw_keys.index((gi, p)) for p in range(RW_HEADS // 2)]
            ss_dn = [_mm(_stack(w_dn[i][hsl(e)], qg2[i][hsl(e)]), s_dn[h])
                     for h, (i, e) in enumerate(dn_i)]
            ss_rw = [_mm_nt(_stack(tl[i], ql[i]), s_rw[p]) for p, i in enumerate(rw_i)]
            v_new = [u_dn[i][hsl(e)] - ss[:GS] for ss, (i, e) in zip(ss_dn, dn_i)]
            for h, (ss, (i, e)) in enumerate(zip(ss_dn, dn_i)):
                o_rows[h][gi] = ss[GS:] + o2_dn[i][hsl(e)]
            us = [ss[:S2] + w2[i] for ss, i in zip(ss_rw, rw_i)]
            for p, (ss, i) in enumerate(zip(ss_rw, rw_i)):
                ys = ss[S2:] + y2[i]
                y_rows[p][gi] = ys[:GS] + ys[GS:]
            u1 = [x[:GS] + x[GS:] for x in us]
            s_dn = [s * e_gl[i][e * GS:e * GS + 1, :] + _mm_tn(k_dec[i][hsl(e)], vn)
                    for s, vn, (i, e) in zip(s_dn, v_new, dn_i)]
            s_rw = [(s + _mm_tn(_stack(v_p[i], -x), _stack(kt_p[i], a_t[i]))) * bd
                    * p_last[gi][0:1, pl_(p)]
                    for p, (s, x, i) in enumerate(zip(s_rw, u1, rw_i))]
        for h in range(DN_HEADS):
            dn_s[h] = s_dn[h]
        for p in range(RW_HEADS // 2):
            rw_s[p] = s_rw[p]

        @pl.when(t == pl.num_programs(1) - 1)
        def _():
            dn_out_ref[0] = dn_s[...]
            for p in range(RW_HEADS // 2):
                rw_out_ref[0, 2 * p] = rw_s[p][:D, :D]
                rw_out_ref[0, 2 * p + 1] = rw_s[p][D:, D:]
    else:
        for gi in range(NG):
            seqs = range(NS)
            rsl = lambda e, j: slice(e * GS + j * C, e * GS + (j + 1) * C)
            dn_items = [(j, h, gd_keys.index((gi, h // 2)), h % 2) for j in seqs for h in range(DN_HEADS)]
            rw_items = [(j, p, rw_keys.index((gi, p))) for j in seqs for p in range(RW_HEADS // 2)]
            s_dn = [dn0_ref[gi * NS + j, h] for j, h, _, _ in dn_items]
            s_rw = [_pair_blockdiag(rw0_ref[gi * NS + j, 2 * p], rw0_ref[gi * NS + j, 2 * p + 1])
                    for j, p, _ in rw_items]
            ss_dn = [_mm(_stack(w_dn[i][rsl(e, j)], qg2[i][rsl(e, j)]), s)
                     for s, (j, h, i, e) in zip(s_dn, dn_items)]
            ss_rw = [_mm_nt(_stack(tl[i][rsl(0, j)], tl[i][rsl(1, j)], ql[i][rsl(0, j)], ql[i][rsl(1, j)]), s)
                     for s, (j, p, i) in zip(s_rw, rw_items)]
            v_new = [u_dn[i][rsl(e, j)] - ss[:C] for ss, (j, h, i, e) in zip(ss_dn, dn_items)]
            o_seq = [ss[C:] + o2_dn[i][rsl(e, j)] for ss, (j, h, i, e) in zip(ss_dn, dn_items)]
            us = [ss[:2 * C] + _stack(w2[i][rsl(0, j)], w2[i][rsl(1, j)]) for ss, (j, p, i) in zip(ss_rw, rw_items)]
            ysq = [ss[2 * C:] + _stack(y2[i][rsl(0, j)], y2[i][rsl(1, j)]) for ss, (j, p, i) in zip(ss_rw, rw_items)]
            u1 = [x[:C] + x[C:] for x in us]
            y_seq = [x[:C] + x[C:] for x in ysq]
            for s, vn, (j, h, i, e) in zip(s_dn, v_new, dn_items):
                row = e * GS + j * C
                dn_out_ref[gi * NS + j, h] = (s * e_gl[i][row:row + 1, :]
                                             + _mm_tn(k_dec[i][rsl(e, j)], vn))
            for s, x, (j, p, i) in zip(s_rw, u1, rw_items):
                sq = slice(j * C, (j + 1) * C)
                s_new = ((s + _mm_tn(_stack(v_p[i][sq], -x), _stack(kt_p[i][sq], a_t[i][sq])))
                         * bd * p_last[gi][j * C:j * C + 1, pl_(p)])
                rw_out_ref[gi * NS + j, 2 * p] = s_new[:D, :D]
                rw_out_ref[gi * NS + j, 2 * p + 1] = s_new[D:, D:]
            for h in range(DN_HEADS):
                parts = [o for o, (j, hh, _, _) in zip(o_seq, dn_items) if hh == h]
                o_rows[h][gi] = parts[0] if NS == 1 else jnp.concatenate(parts, axis=0)
            for p in range(RW_HEADS // 2):
                parts = [y for y, (j, pp, _) in zip(y_seq, rw_items) if pp == p]
                y_rows[p][gi] = parts[0] if NS == 1 else jnp.concatenate(parts, axis=0)

    cat = lambda parts: parts[0] if len(parts) == 1 else jnp.concatenate(parts, axis=0)
    dn_w = dnnorm_ref[...]
    for h in range(DN_HEADS):
        zh = z[:, h * LANES:(h + 1) * LANES]
        put(slice(h * LANES, (h + 1) * LANES), _rms(cat(o_rows[h]), dn_w) * (zh * _sigmoid(zh)))
    for p in range(RW_HEADS // 2):
        sl = pl_(p)
        y = cat(y_rows[p])
        mu_g = gsum(y) * (1.0 / RW_HEAD_DIM)
        yc = y - mu_g
        var = gsum(yc * yc) * (1.0 / RW_HEAD_DIM)
        yn = yc * lax.rsqrt(var + GN_EPS) * ln_w[:, sl] + ln_b[:, sl]
        bonus = gsum(bonus_in[:, sl]) * vr[:, sl]
        put(slice(DN_WIDTH + p * LANES, DN_WIDTH + (p + 1) * LANES), (yn + bonus) * g[:, sl])


def _mixer(proj, dn0, rw0, lp, *, batch, seq, rows, group, chunk, n_pad, carry, layer=None, hist=(),
           acc=()):
    full = lambda shape: pl.BlockSpec(shape, lambda b, t: (0,) * len(shape))
    dn_blk = (DN_HEADS, DN_HEAD_DIM, DN_HEAD_DIM)
    rw_blk = (RW_HEADS, RW_HEAD_DIM, RW_HEAD_DIM)
    if carry:
        nt = seq // rows
        grid = (batch, nt)
        bsel = (lambda b: b) if dn0.shape[0] == batch else (lambda b: 0)
        row_map = lambda b, t: (b * nt + t, 0)
        st_in = [pl.BlockSpec((1,) + blk, lambda b, t: (bsel(b), 0, 0, 0)) for blk in (dn_blk, rw_blk)]
        st_out = [pl.BlockSpec((1,) + blk, lambda b, t: (b, 0, 0, 0)) for blk in (dn_blk, rw_blk)]
        scratch = [pltpu.VMEM((DN_HEADS, LANES, LANES), F32), pltpu.VMEM((RW_HEADS // 2, LANES, LANES), F32)]
        io_rows = rows
    else:
        nseq = rows // chunk
        grid = (batch // nseq, 1)
        row_map = lambda b, t: (b, 0)
        io_rows = nseq * (chunk - n_pad)
        st_in = [pl.BlockSpec((None, nseq) + blk, lambda b, t: (layer, b, 0, 0, 0)) for blk in (dn_blk, rw_blk)]
        st_in += [pl.BlockSpec(memory_space=pl.ANY)] * len(acc)
        st_out = [pl.BlockSpec((None, nseq) + blk, lambda b, t: (layer, b, 0, 0, 0)) for blk in (dn_blk, rw_blk)]
        scratch = []
    state_shapes = ([jax.ShapeDtypeStruct(a.shape, F32) for a in acc] if acc else
                    [jax.ShapeDtypeStruct((batch,) + blk, F32) for blk in (dn_blk, rw_blk)])
    first_acc = 1 + len(hist) + 2
    return pl.pallas_call(
        functools.partial(_mixer_body, rows=rows, group=group, chunk=chunk, n_pad=n_pad, carry=carry),
        grid=grid,
        in_specs=[
            pl.BlockSpec((io_rows, P_COLS), row_map),
            *[pl.BlockSpec(h.shape[:-2] + (rows // chunk, h.shape[-1]), lambda b, t, n=h.ndim: (0,) * (n - 2) + (b, 0))
              for h in hist],
            *st_in,
            full((SUBLANES, DN_QKV)), full((SUBLANES, LANES)), full((1, LANES)),
            full((1, RW_COLS)), full((SUBLANES, RW_WIDTH)),
            full((LANES, RW_WIDTH)), full((LANES, RW_WIDTH)), full((G_LORA, RW_WIDTH)),
        ],
        out_specs=[pl.BlockSpec((io_rows, D_MODEL), row_map), *st_out],
        out_shape=[jax.ShapeDtypeStruct((proj.shape[0], D_MODEL), F32), *state_shapes],
        input_output_aliases={first_acc + i: 1 + i for i in range(len(acc))},
        scratch_shapes=scratch,
        compiler_params=pltpu.CompilerParams(
            dimension_semantics=("parallel", "arbitrary"), vmem_limit_bytes=VMEM_LIMIT),
        name="mixer",
    )(proj, *hist, dn0, rw0, *acc, lp["conv_w"], lp["gdnp"], lp["dn_norm"], lp["mu"],
      lp["rwvec"], lp["w_up"], lp["a_up"], lp["g_up"])


def _prep_layer_params(l, w_in, dn_conv_w, dn_a_log, dn_dt_bias, dn_norm_w, rw_mu, rw_w0, rw_w_up,
                       rw_a0, rw_a_up, rw_g_up, rw_k_k, rw_k_a, rw_r_k, rw_ln_w, rw_ln_b):
    off = DN_QKV + DN_WIDTH
    wi = w_in[l].astype(BF16)
    w_in_p = jnp.concatenate(
        [wi[:, :off + 2 * DN_HEADS], jnp.zeros((D_MODEL, LANES - 2 * DN_HEADS), BF16),
         wi[:, off + 2 * DN_HEADS:]], axis=1)
    gdnp = jnp.zeros((SUBLANES, LANES), F32)
    gdnp = gdnp.at[0, DN_HEADS:2 * DN_HEADS].set(dn_dt_bias[l])
    gdnp = gdnp.at[1, DN_HEADS:2 * DN_HEADS].set(dn_a_log[l])
    rwvec = jnp.stack([rw_w0[l], rw_a0[l], rw_k_k[l], rw_k_a[l], rw_r_k[l].reshape(RW_WIDTH),
                       rw_ln_w[l], rw_ln_b[l], jnp.zeros((RW_WIDTH,), F32)])
    zeros_lora = jnp.zeros((W_LORA, RW_WIDTH), F32)
    return {
        "w_in": w_in_p,
        "conv_w": jnp.concatenate([dn_conv_w[l], jnp.zeros((SUBLANES - CONV_W, DN_QKV), F32)], axis=0),
        "gdnp": gdnp,
        "dn_norm": dn_norm_w[l].reshape(1, LANES),
        "mu": rw_mu[l].reshape(1, RW_COLS),
        "rwvec": rwvec,
        "w_up": jnp.concatenate([rw_w_up[l], zeros_lora], axis=0),
        "a_up": jnp.concatenate([zeros_lora, rw_a_up[l]], axis=0),
        "g_up": rw_g_up[l],
    }


def _post_layer(x, mixed, ffn, l):
    return _post(x, mixed, ffn["w_out"][l], ffn["g_post_mix"][l], ffn["g_pre_ffn"][l],
                 ffn["w_gate"][l], ffn["w_up"][l], ffn["w_down"][l], ffn["g_post_ffn"][l])


def _trunk_carry(x, states, layers, ffn, *, batch, seq, rows, group):
    new_states = []
    for l in range(DEPTH):
        conv_s, dn_s, shift_s, rw_s = states[l]
        prep, conv_t, shift_t = _inproj_prep(x, ffn["g_pre_mix"][l], layers[l]["w_in"], conv_s, shift_s,
                                             layers[l], batch=batch, seq=seq, rows=rows)
        mixed, dn_new, rw_new = _mixer(prep, dn_s, rw_s, layers[l], batch=batch, seq=seq, rows=rows,
                                       group=group, chunk=group, n_pad=0, carry=True)
        x = _post_layer(x, mixed, ffn, l)
        new_states.append((conv_t, dn_new, shift_t, rw_new))
    return x, new_states


def _trunk_sample(x, state_dn_conv, state_dn, state_rw_shift, state_rw, layers, ffn, *, batch, seq_real):
    seq = SUBLANES
    n_pad = seq - seq_real
    assert n_pad >= CONV_W - 1, "the masked front rows must hold the whole conv history"
    conv_new, shift_new = [], []
    dn_acc = jnp.zeros(state_dn.shape, F32)
    rw_acc = jnp.zeros(state_rw.shape, F32)
    for l in range(DEPTH):
        proj = _inproj(x, ffn["g_pre_mix"][l], layers[l]["w_in"])
        hist = (jnp.transpose(state_dn_conv[l], (1, 0, 2)), state_rw_shift[l])
        mixed, dn_acc, rw_acc = _mixer(proj, state_dn, state_rw, layers[l], batch=batch, seq=seq,
                                       rows=SAMPLE_ROWS, group=GROUP, chunk=seq, n_pad=n_pad, carry=False,
                                       layer=l, hist=hist, acc=(dn_acc, rw_acc))
        x = _post_layer(x, mixed, ffn, l)
        conv_new.append(proj[:, C_QKV:C_QKV + DN_QKV].reshape(batch, seq_real, DN_QKV)[:, 1 - CONV_W:])
        shift_new.append(proj[:, C_RW:].reshape(batch, seq_real, RW_COLS)[:, -1])
    return x, (jnp.stack(conv_new), dn_acc, jnp.stack(shift_new), rw_acc)


def kernel(x_prompt, x_sample, state_dn_conv, state_dn, state_rw_shift, state_rw, meta, g_pre_mix, g_post_mix, g_pre_ffn, g_post_ffn, w_in, dn_conv_w, dn_a_log, dn_dt_bias, dn_norm_w, rw_mu, rw_w0, rw_w_up, rw_a0, rw_a_up, rw_g_up, rw_k_k, rw_k_a, rw_r_k, rw_ln_w, rw_ln_b, w_out, w_gate, w_up, w_down):
    batch, seq, _ = x_prompt.shape
    dec_batch, dec_seq, _ = x_sample.shape
    layers = [_prep_layer_params(l, w_in, dn_conv_w, dn_a_log, dn_dt_bias, dn_norm_w, rw_mu, rw_w0,
                                 rw_w_up, rw_a0, rw_a_up, rw_g_up, rw_k_k, rw_k_a, rw_r_k, rw_ln_w,
                                 rw_ln_b) for l in range(DEPTH)]
    vec = lambda g: g.reshape(DEPTH, 1, D_MODEL)
    ffn = {"g_pre_mix": vec(g_pre_mix), "g_post_mix": vec(g_post_mix), "g_pre_ffn": vec(g_pre_ffn),
           "g_post_ffn": vec(g_post_ffn), "w_out": w_out.astype(BF16), "w_gate": w_gate.astype(BF16),
           "w_up": w_up.astype(BF16), "w_down": w_down.astype(BF16)}

    zero_states = [(jnp.zeros((1, SUBLANES, DN_QKV), F32), jnp.zeros((1, DN_HEADS, DN_HEAD_DIM, DN_HEAD_DIM), F32),
                    jnp.zeros((1, SUBLANES, RW_COLS), F32),
                    jnp.zeros((1, RW_HEADS, RW_HEAD_DIM, RW_HEAD_DIM), F32))] * DEPTH
    _, meta_states = _trunk_carry(meta.astype(F32), zero_states, layers, ffn,
                                  batch=1, seq=N_META, rows=N_META, group=N_META)

    yp, ps = _trunk_carry(x_prompt.reshape(batch * seq, D_MODEL), meta_states, layers, ffn,
                          batch=batch, seq=seq, rows=PROMPT_ROWS, group=GROUP)
    ps = [(c[:, SUBLANES - (CONV_W - 1):], d, s[:, SUBLANES - 1], r) for c, d, s, r in ps]

    ys, (cs, ds, sh, rs) = _trunk_sample(x_sample.reshape(dec_batch * dec_seq, D_MODEL), state_dn_conv,
                                         state_dn, state_rw_shift, state_rw, layers, ffn,
                                         batch=dec_batch, seq_real=dec_seq)

    cp, dp, sp, rp = (jnp.stack([s[i] for s in ps]) for i in range(4))
    return (yp.reshape(batch, seq, D_MODEL), ys.reshape(dec_batch, dec_seq, D_MODEL),
            cp, dp, sp, rp, cs, ds, sh, rs)
```

```python
import functools
import math

import jax
import jax.numpy as jnp
from jax import lax
from jax.experimental import pallas as pl
from jax.experimental.pallas import tpu as pltpu

F32 = jnp.float32
BF16 = jnp.bfloat16

D_MODEL = 1024
DEPTH = 4
N_META = 16
DN_HEADS = 4
DN_HEAD_DIM = 128
DN_WIDTH = DN_HEADS * DN_HEAD_DIM
DN_QKV = 3 * DN_WIDTH
RW_HEADS = 8
RW_HEAD_DIM = 64
RW_WIDTH = RW_HEADS * RW_HEAD_DIM
W_LORA = 64
A_LORA = 64
G_LORA = 128
RW_COLS = 3 * RW_WIDTH + W_LORA + A_LORA + G_LORA
D_FF = 2816
CONV_W = 4
RMS_EPS = 1e-6
L2_EPS = 1e-6
GN_EPS = 64e-5

LANES = 128
SUBLANES = 8
VMEM_LIMIT = 56 * 1024 * 1024

C_QKV = 0
C_Z = DN_QKV
C_BG = C_Z + DN_WIDTH
C_RW = C_BG + LANES
P_COLS = C_RW + RW_COLS
GROUP = 64
PROMPT_ROWS = 512
SAMPLE_ROWS = 128


def _sigmoid(x):
    return 1.0 / (1.0 + jnp.exp(-x))


def _softplus(x):
    return jnp.maximum(x, 0.0) + jnp.log(1.0 + jnp.exp(-jnp.abs(x)))


def _rms(x, g):
    return x * lax.rsqrt(jnp.mean(x * x, axis=-1, keepdims=True) + RMS_EPS) * g


def _mm(a, b):
    return jnp.dot(a.astype(BF16), b.astype(BF16), preferred_element_type=F32)


def _mm_nt(a, b):
    return lax.dot_general(a.astype(BF16), b.astype(BF16), (((1,), (1,)), ((), ())),
                           preferred_element_type=F32)


def _mm_tn(a, b):
    return lax.dot_general(a.astype(BF16), b.astype(BF16), (((0,), (0,)), ((), ())),
                           preferred_element_type=F32)


def _mm_sel(sel, x):
    sel = sel.astype(BF16)
    x1 = x.astype(BF16)
    x2 = (x - x1.astype(F32)).astype(BF16)
    dot = lambda t: jnp.dot(sel, t, preferred_element_type=F32)
    return dot(x1) + dot(x2)


def _bdot(a, b):
    return jnp.dot(a.astype(BF16), b, preferred_element_type=F32)


def _inproj_body(x_ref, g_ref, w_ref, o_ref):
    h = _rms(x_ref[...], g_ref[...])
    o_ref[...] = _bdot(h, w_ref[...])


def _inproj(x, g, w):
    rows = x.shape[0]
    tm = min(rows, 256)
    return pl.pallas_call(
        _inproj_body,
        grid=(rows // tm,),
        in_specs=[
            pl.BlockSpec((tm, D_MODEL), lambda i: (i, 0)),
            pl.BlockSpec((1, D_MODEL), lambda i: (0, 0)),
            pl.BlockSpec((D_MODEL, P_COLS), lambda i: (0, 0)),
        ],
        out_specs=pl.BlockSpec((tm, P_COLS), lambda i: (i, 0)),
        out_shape=jax.ShapeDtypeStruct((rows, P_COLS), F32),
        compiler_params=pltpu.CompilerParams(
            dimension_semantics=("parallel",), vmem_limit_bytes=VMEM_LIMIT),
        name="inproj",
    )(x, g, w)


def _row_prep(proj, halo_qkv, halo_rw, cw, mu, rows):
    H = SUBLANES
    R = rows
    x_qkv = proj[:, C_QKV:C_QKV + DN_QKV]
    ext = jnp.concatenate([halo_qkv, x_qkv], axis=0)
    back = lambda a, k: pltpu.roll(a, k, 0)[H:H + R, :]
    conv = (back(ext, 3) * cw[0:1, :] + back(ext, 2) * cw[1:2, :] + back(ext, 1) * cw[2:3, :]
            + x_qkv * cw[3:4, :])
    half = 0.5 * conv
    qkv = half + half * jnp.tanh(half)

    def l2n(x):
        return x * lax.rsqrt(jnp.sum(x * x, axis=-1, keepdims=True) + L2_EPS)

    hl = lambda base, h: slice(base + h * LANES, base + (h + 1) * LANES)
    parts = ([l2n(qkv[:, hl(0, h)]) * (DN_HEAD_DIM ** -0.5) for h in range(DN_HEADS)]
             + [l2n(qkv[:, hl(DN_WIDTH, h)]) for h in range(DN_HEADS)]
             + [qkv[:, hl(2 * DN_WIDTH, h)] for h in range(DN_HEADS)])
    cols = proj[:, C_RW:C_RW + RW_COLS]
    prev = back(jnp.concatenate([halo_rw, cols], axis=0), 1)
    xs = cols + (prev - cols) * mu
    return parts, xs, x_qkv[R - H:R, :], cols[R - H:R, :]


def _inproj_prep_body(x_ref, g_ref, w_ref, conv0_ref, shift0_ref, cw_ref, mu_ref,
                      o_ref, ctail_ref, stail_ref, halo_qkv, halo_rw, *, rows):
    @pl.when(pl.program_id(1) == 0)
    def _():
        halo_qkv[...] = conv0_ref[0]
        halo_rw[...] = shift0_ref[0]

    proj = _bdot(_rms(x_ref[...], g_ref[...]), w_ref[...])
    parts, xs, tail_qkv, tail_rw = _row_prep(proj, halo_qkv[...], halo_rw[...], cw_ref[...],
                                             mu_ref[...], rows)
    for idx, part in enumerate(parts):
        o_ref[:, idx * LANES:(idx + 1) * LANES] = part
    o_ref[:, C_Z:C_RW] = proj[:, C_Z:C_RW]
    o_ref[:, C_RW:C_RW + RW_COLS] = xs
    halo_qkv[...] = tail_qkv
    halo_rw[...] = tail_rw
    ctail_ref[0] = tail_qkv
    stail_ref[0] = tail_rw


def _inproj_prep(x, g, w, conv0, shift0, lp, *, batch, seq, rows):
    nt = seq // rows
    bsel = (lambda b: b) if conv0.shape[0] == batch else (lambda b: 0)
    full = lambda shape: pl.BlockSpec(shape, lambda b, t: (0,) * len(shape))
    return pl.pallas_call(
        functools.partial(_inproj_prep_body, rows=rows),
        grid=(batch, nt),
        in_specs=[
            pl.BlockSpec((rows, D_MODEL), lambda b, t: (b * nt + t, 0)),
            full((1, D_MODEL)), full((D_MODEL, P_COLS)),
            pl.BlockSpec((1, SUBLANES, DN_QKV), lambda b, t: (bsel(b), 0, 0)),
            pl.BlockSpec((1, SUBLANES, RW_COLS), lambda b, t: (bsel(b), 0, 0)),
            full((SUBLANES, DN_QKV)), full((1, RW_COLS)),
        ],
        out_specs=[
            pl.BlockSpec((rows, P_COLS), lambda b, t: (b * nt + t, 0)),
            pl.BlockSpec((1, SUBLANES, DN_QKV), lambda b, t: (b, 0, 0)),
            pl.BlockSpec((1, SUBLANES, RW_COLS), lambda b, t: (b, 0, 0)),
        ],
        out_shape=[jax.ShapeDtypeStruct((batch * seq, P_COLS), F32),
                   jax.ShapeDtypeStruct((batch, SUBLANES, DN_QKV), F32),
                   jax.ShapeDtypeStruct((batch, SUBLANES, RW_COLS), F32)],
        scratch_shapes=[pltpu.VMEM((SUBLANES, DN_QKV), F32), pltpu.VMEM((SUBLANES, RW_COLS), F32)],
        compiler_params=pltpu.CompilerParams(
            dimension_semantics=("parallel", "arbitrary"), vmem_limit_bytes=VMEM_LIMIT),
        name="inproj_prep",
    )(x, g, w, conv0, shift0, lp["conv_w"], lp["mu"])


FF_CHUNK = 256


def _post_body(x_ref, m_ref, wo_ref, g1_ref, g2_ref, wg_ref, wu_ref, wd_ref, g3_ref, o_ref):
    mixed = _bdot(m_ref[...], wo_ref[...])
    x1 = x_ref[...] + _rms(mixed, g1_ref[...])
    h = _rms(x1, g2_ref[...]).astype(BF16)
    acc = jnp.zeros(x1.shape, F32)
    for c in range(D_FF // FF_CHUNK):
        sl = slice(c * FF_CHUNK, (c + 1) * FF_CHUNK)
        gate = jnp.dot(h, wg_ref[:, sl], preferred_element_type=F32)
        up = jnp.dot(h, wu_ref[:, sl], preferred_element_type=F32)
        act = gate * _sigmoid(gate) * up
        acc = acc + _bdot(act, wd_ref[sl, :])
    o_ref[...] = x1 + _rms(acc, g3_ref[...])


def _post(x, mixed, wo, g1, g2, wg, wu, wd, g3):
    rows = x.shape[0]
    tm = min(rows, 512)
    row_spec = pl.BlockSpec((tm, D_MODEL), lambda i: (i, 0))
    vec_spec = pl.BlockSpec((1, D_MODEL), lambda i: (0, 0))

    def resident(shape):
        return pl.BlockSpec(shape, lambda i: (0, 0), pipeline_mode=pl.Buffered(1))

    return pl.pallas_call(
        _post_body,
        grid=(rows // tm,),
        in_specs=[row_spec, row_spec, resident((D_MODEL, D_MODEL)), vec_spec, vec_spec,
                  resident((D_MODEL, D_FF)), resident((D_MODEL, D_FF)), resident((D_FF, D_MODEL)),
                  vec_spec],
        out_specs=row_spec,
        out_shape=jax.ShapeDtypeStruct((rows, D_MODEL), F32),
        compiler_params=pltpu.CompilerParams(
            dimension_semantics=("parallel",), vmem_limit_bytes=VMEM_LIMIT),
        name="post",
    )(x, mixed, wo, g1, g2, wg, wu, wd, g3)


def _inv_unit_lower_many(mats, eye, chunk):
    ns = [-a for a in mats]
    ps = [eye + n for n in ns]
    for _ in range(max(int(math.ceil(math.log2(chunk))) - 1, 0)):
        ns = [_mm(n, n) for n in ns]
        ps = [p + _mm(p, n) for p, n in zip(ps, ns)]
    return ps


def _stack(*parts):
    return jnp.concatenate(parts, axis=0)


def _pair_blockdiag(s0, s1):
    zero = jnp.zeros_like(s0)
    return jnp.concatenate([jnp.concatenate([s0, zero], axis=1),
                            jnp.concatenate([zero, s1], axis=1)], axis=0)


def _mixer_body(proj_ref, *refs, rows, group, chunk, n_pad, carry):
    if carry:
        (dn0_ref, rw0_ref, convw_ref, gdnp_ref, dnnorm_ref, mu_ref, rwvec_ref, wup_ref, aup_ref, gup_ref,
         mixed_ref, dn_out_ref, rw_out_ref, *scratch) = refs
    else:
        (hconv_ref, hshift_ref, dn0_ref, rw0_ref, _, _, convw_ref, gdnp_ref, dnnorm_ref, mu_ref, rwvec_ref,
         wup_ref, aup_ref, gup_ref, mixed_ref, dn_out_ref, rw_out_ref, *scratch) = refs
    R, GS, C = rows, group, chunk
    NG = R // GS
    NS = GS // C
    S2 = 2 * GS
    H = SUBLANES
    shift = int(math.log2(C))
    t = pl.program_id(1)

    D = RW_HEAD_DIM
    if carry:
        dn_s, rw_s = scratch

        @pl.when(t == 0)
        def _():
            dn_s[...] = dn0_ref[0]
            for p in range(RW_HEADS // 2):
                rw_s[p] = _pair_blockdiag(rw0_ref[0, 2 * p], rw0_ref[0, 2 * p + 1])

    ri = lax.broadcasted_iota(jnp.int32, (S2, S2), 0)
    ci = lax.broadcasted_iota(jnp.int32, (S2, S2), 1)
    same = (ri >> shift) == (ci >> shift)
    strict = same & (ci < ri)
    incl = same & (ci <= ri)
    eye2 = (ri == ci).astype(F32)
    r1 = lax.broadcasted_iota(jnp.int32, (GS, GS), 0)
    c1 = lax.broadcasted_iota(jnp.int32, (GS, GS), 1)
    lt_blk = (((r1 >> shift) == (c1 >> shift)) & (c1 <= r1)).astype(F32)
    last_sel = (c1 == (((r1 >> shift) << shift) + (C - 1))).astype(F32)
    lane = lax.broadcasted_iota(jnp.int32, (1, LANES), 1)
    m0 = (lane < RW_HEAD_DIM).astype(F32)
    m1 = 1.0 - m0

    def gsum(x):
        if carry:
            return _mm(x, bd)
        s0 = jnp.sum(x * m0, axis=-1, keepdims=True)
        s1 = jnp.sum(x * m1, axis=-1, keepdims=True)
        return jnp.where(lane < RW_HEAD_DIM, s0, s1)
    bi = lax.broadcasted_iota(jnp.int32, (LANES, LANES), 0)
    bj = lax.broadcasted_iota(jnp.int32, (LANES, LANES), 1)
    bd = ((bi >= RW_HEAD_DIM) == (bj >= RW_HEAD_DIM)).astype(F32)
    if n_pad:
        rowm = ((lax.broadcasted_iota(jnp.int32, (R, 1), 0) & (C - 1)) >= n_pad).astype(F32)
    else:
        rowm = None
    grp = [slice(g * GS, (g + 1) * GS) for g in range(NG)]

    def block_last(x):
        if NS == 1:
            return jnp.broadcast_to(x[GS - 1:GS, :], x.shape)
        return _mm_sel(last_sel, x)

    hl = lambda base, h: slice(base + h * LANES, base + (h + 1) * LANES)
    if carry:
        qn = [proj_ref[:, hl(0, h)] for h in range(DN_HEADS)]
        kn = [proj_ref[:, hl(DN_WIDTH, h)] for h in range(DN_HEADS)]
        vv = [proj_ref[:, hl(2 * DN_WIDTH, h)] for h in range(DN_HEADS)]
        xs = proj_ref[:, C_RW:C_RW + RW_COLS]
        z = proj_ref[:, C_Z:C_Z + DN_WIDTH]
        bg = proj_ref[:, C_BG:C_BG + LANES]
        put = lambda lanes, val: mixed_ref.__setitem__((slice(None), lanes), val)
    else:
        n_real = C - n_pad
        pr = lax.broadcasted_iota(jnp.int32, (R, R * n_real // C), 0)
        rr = lax.broadcasted_iota(jnp.int32, (R, R * n_real // C), 1)
        same_seq = (pr >> shift) == (rr >> int(math.log2(n_real)))
        e_tok = (same_seq & ((pr & (C - 1)) - n_pad == (rr & (n_real - 1)))).astype(F32)
        pc = lax.broadcasted_iota(jnp.int32, (R * n_real // C, R), 1)
        rc = lax.broadcasted_iota(jnp.int32, (R * n_real // C, R), 0)
        e_back = (((pc >> shift) == (rc >> int(math.log2(n_real))))
                  & ((pc & (C - 1)) - n_pad == (rc & (n_real - 1)))).astype(F32)
        ps = lax.broadcasted_iota(jnp.int32, (R, R // C), 0)
        sq = lax.broadcasted_iota(jnp.int32, (R, R // C), 1)
        e_row = lambda k: (((ps >> shift) == sq) & ((ps & (C - 1)) == n_pad - k)).astype(F32)
        tile_qkv = _mm_sel(e_tok, proj_ref[:, C_QKV:C_QKV + DN_QKV])
        for i in range(CONV_W - 1):
            tile_qkv = tile_qkv + _mm_sel(e_row(CONV_W - 1 - i), hconv_ref[i])
        tile_rw = (_mm_sel(e_tok, proj_ref[:, C_RW:C_RW + RW_COLS]) + _mm_sel(e_row(1), hshift_ref[...]))
        zbg = _mm_sel(e_tok, proj_ref[:, C_Z:C_RW])
        z = zbg[:, :DN_WIDTH]
        bg = zbg[:, DN_WIDTH:]
        parts, xs, _, _ = _row_prep(jnp.concatenate([tile_qkv, zbg, tile_rw], axis=1),
                                    jnp.zeros((H, DN_QKV), F32), jnp.zeros((H, RW_COLS), F32),
                                    convw_ref[...], mu_ref[...], R)
        qn, kn, vv = parts[:DN_HEADS], parts[DN_HEADS:2 * DN_HEADS], parts[2 * DN_HEADS:]
        put = lambda lanes, val: mixed_ref.__setitem__((slice(None), lanes), _mm_sel(e_back, val))

    gp = gdnp_ref[...]
    beta_all = _sigmoid(bg)
    glog_all = -jnp.exp(gp[1:2, :]) * _softplus(bg + gp[0:1, :])
    if rowm is not None:
        beta_all = beta_all * rowm
        glog_all = glog_all * rowm
    gcum = [_mm_sel(lt_blk, glog_all[rs]) for rs in grp]
    glast = [block_last(gc) for gc in gcum]

    r = xs[:, 0:RW_WIDTH]
    kr = xs[:, RW_WIDTH:2 * RW_WIDTH]
    vr = xs[:, 2 * RW_WIDTH:3 * RW_WIDTH]
    wa = xs[:, 3 * RW_WIDTH:3 * RW_WIDTH + LANES]
    gd = xs[:, 3 * RW_WIDTH + LANES:RW_COLS]
    rv = rwvec_ref[...]
    w0, a0, k_k, k_a, r_k, ln_w, ln_b = (rv[i:i + 1, :] for i in range(7))
    w_log = -_softplus(-(w0 + _mm(jnp.tanh(wa), wup_ref[...]))) - 0.5
    lw = -jnp.exp(w_log)
    a = _sigmoid(a0 + _mm(wa, aup_ref[...]))
    g = _mm(_sigmoid(gd), gup_ref[...])
    kk_raw = kr * k_k
    kr2 = kr * (1.0 + (a - 1.0) * k_a)
    bonus_in = r * kr2 * r_k
    if rowm is not None:
        lw = lw * rowm
        kr2 = kr2 * rowm
    cl_g = [_mm_sel(lt_blk, lw[rs]) for rs in grp]
    cl = cl_g[0] if NG == 1 else jnp.concatenate(cl_g, axis=0)
    p_last = [jnp.exp(block_last(c)) for c in cl_g]
    e_cl = jnp.exp(cl)
    e_prev = jnp.exp(cl - lw)
    e_inv = jnp.exp(-cl)
    r_t = r * e_cl
    k_t = kr2 * e_inv
    pl_ = lambda p: slice(p * LANES, (p + 1) * LANES)
    kk = []
    for p in range(RW_HEADS // 2):
        kp = kk_raw[:, pl_(p)]
        kp = kp * lax.rsqrt(gsum(kp * kp) + L2_EPS)
        if rowm is not None:
            kp = kp * rowm
        kk.append(kp)

    gd_keys = [(gi, p) for gi in range(NG) for p in range(DN_HEADS // 2)]

    def gstack(arrs, gi, p):
        return _stack(arrs[2 * p][grp[gi]], arrs[2 * p + 1][grp[gi]])

    def gcol(mat, col, p):
        return _stack(mat[:, col + 2 * p:col + 2 * p + 1], mat[:, col + 2 * p + 1:col + 2 * p + 2])

    q2 = [gstack(qn, gi, p) for gi, p in gd_keys]
    k2 = [gstack(kn, gi, p) for gi, p in gd_keys]
    v2 = [gstack(vv, gi, p) for gi, p in gd_keys]
    beta2 = [gcol(beta_all[grp[gi]], 0, p) for gi, p in gd_keys]
    gc2 = [gcol(gcum[gi], DN_HEADS, p) for gi, p in gd_keys]
    gl2 = [gcol(glast[gi], DN_HEADS, p) for gi, p in gd_keys]
    gct = [jnp.transpose(gc) for gc in gcum]
    rb = [jnp.concatenate([gct[gi][DN_HEADS + 2 * p:DN_HEADS + 2 * p + 1, :],
                           gct[gi][DN_HEADS + 2 * p + 1:DN_HEADS + 2 * p + 2, :]], axis=1)
          for gi, p in gd_keys]
    dec = [jnp.exp(jnp.where(incl, c - b, 0.0)) for c, b in zip(gc2, rb)]
    kb = [k * b for k, b in zip(k2, beta2)]
    s_kk = [_mm_nt(x, k) for x, k in zip(kb, k2)]
    s_qk = [_mm_nt(x, k) for x, k in zip(q2, k2)]
    a_mat = [jnp.where(strict, s * d, 0.0) for s, d in zip(s_kk, dec)]
    qk = [jnp.where(incl, s * d, 0.0) for s, d in zip(s_qk, dec)]
    t_dn = _inv_unit_lower_many(a_mat, eye2, C)
    eg = [jnp.exp(c) for c in gc2]
    uw = [_mm(tm, jnp.concatenate([v * b, x * e], axis=1))
          for tm, v, b, x, e in zip(t_dn, v2, beta2, kb, eg)]
    qo = [_mm(m, x) for m, x in zip(qk, uw)]
    u_dn = [x[:, :LANES] for x in uw]
    w_dn = [x[:, LANES:] for x in uw]
    o2_dn = [x[:, :LANES] for x in qo]
    qg2 = [q * e - x[:, LANES:] for q, e, x in zip(q2, eg, qo)]
    k_dec = [k * jnp.exp(l - c) for k, l, c in zip(k2, gl2, gc2)]
    e_gl = [jnp.exp(l) for l in gl2]

    rw_keys = [(gi, p) for gi in range(NG) for p in range(RW_HEADS // 2)]
    kk_t = [kk[p][grp[gi]] * e_prev[grp[gi], pl_(p)] for gi, p in rw_keys]
    a_t = [kk[p][grp[gi]] * a[grp[gi], pl_(p)] * e_inv[grp[gi], pl_(p)] for gi, p in rw_keys]
    rt_p = [r_t[grp[gi], pl_(p)] for gi, p in rw_keys]
    kt_p = [k_t[grp[gi], pl_(p)] for gi, p in rw_keys]
    v_p = [vr[grp[gi], pl_(p)] for gi, p in rw_keys]
    l_kk = [_stack(x * m0, x * m1) for x in kk_t]
    l_r = [_stack(x * m0, x * m1) for x in rt_p]
    vs = [_stack(x * m0, x * m1) for x in v_p]
    if S2 % LANES == 0:
        sc = [_mm_nt(_stack(lk, lr), _stack(x, x, y, y))
              for lk, lr, x, y in zip(l_kk, l_r, a_t, kt_p)]
        m_ka = [jnp.where(strict, s[:S2, :S2], 0.0) for s in sc]
        m_kv = [jnp.where(strict, s[:S2, S2:], 0.0) for s in sc]
        a_ra = [jnp.where(incl, s[S2:, :S2], 0.0) for s in sc]
        a_rk = [jnp.where(incl, s[S2:, S2:], 0.0) for s in sc]
    else:
        ra2 = [_stack(x, x) for x in a_t]
        rk2 = [_stack(x, x) for x in kt_p]
        m_ka = [jnp.where(strict, _mm_nt(x, y), 0.0) for x, y in zip(l_kk, ra2)]
        m_kv = [jnp.where(strict, _mm_nt(x, y), 0.0) for x, y in zip(l_kk, rk2)]
        a_ra = [jnp.where(incl, _mm_nt(x, y), 0.0) for x, y in zip(l_r, ra2)]
        a_rk = [jnp.where(incl, _mm_nt(x, y), 0.0) for x, y in zip(l_r, rk2)]
    t_rw = _inv_unit_lower_many(m_ka, eye2, C)
    mv = [_mm(m, x) for m, x in zip(m_kv, vs)]
    tlw = [_mm(tm, jnp.concatenate([lk, x], axis=1)) for tm, lk, x in zip(t_rw, l_kk, mv)]
    aa = [_mm(m, x) for m, x in zip(a_ra, tlw)]
    av = [_mm(m, x) for m, x in zip(a_rk, vs)]
    tl = [x[:, :LANES] for x in tlw]
    w2 = [x[:, LANES:] for x in tlw]
    ql = [lr - x[:, :LANES] for lr, x in zip(l_r, aa)]
    y2 = [x - y[:, LANES:] for x, y in zip(av, aa)]

    o_rows = [[None] * NG for _ in range(DN_HEADS)]
    y_rows = [[None] * NG for _ in range(RW_HEADS // 2)]
    if carry:
        s_dn = [dn_s[h] for h in range(DN_HEADS)]
        s_rw = [rw_s[p] for p in range(RW_HEADS // 2)]
        for gi in range(NG):
            hsl = lambda e: slice(e * GS, (e + 1) * GS)
            dn_i = [(gd_keys.index((gi, h // 2)), h % 2) for h in range(DN_HEADS)]
            rw_i = [rw_keys.index((gi, p)) for p in range(RW_HEADS // 2)]
            ss_dn = [_mm(_stack(w_dn[i][hsl(e)], qg2[i][hsl(e)]), s_dn[h])
                     for h, (i, e) in enumerate(dn_i)]
            ss_rw = [_mm_nt(_stack(tl[i], ql[i]), s_rw[p]) for p, i in enumerate(rw_i)]
            v_new = [u_dn[i][hsl(e)] - ss[:GS] for ss, (i, e) in zip(ss_dn, dn_i)]
            for h, (ss, (i, e)) in enumerate(zip(ss_dn, dn_i)):
                o_rows[h][gi] = ss[GS:] + o2_dn[i][hsl(e)]
            us = [ss[:S2] + w2[i] for ss, i in zip(ss_rw, rw_i)]
            for p, (ss, i) in enumerate(zip(ss_rw, rw_i)):
                ys = ss[S2:] + y2[i]
                y_rows[p][gi] = ys[:GS] + ys[GS:]
            u1 = [x[:GS] + x[GS:] for x in us]
            s_dn = [s * e_gl[i][e * GS:e * GS + 1, :] + _mm_tn(k_dec[i][hsl(e)], vn)
                    for s, vn, (i, e) in zip(s_dn, v_new, dn_i)]
            s_rw = [(s + _mm_tn(_stack(v_p[i], -x), _stack(kt_p[i], a_t[i]))) * bd
                    * p_last[gi][0:1, pl_(p)]
                    for p, (s, x, i) in enumerate(zip(s_rw, u1, rw_i))]
        for h in range(DN_HEADS):
            dn_s[h] = s_dn[h]
        for p in range(RW_HEADS // 2):
            rw_s[p] = s_rw[p]

        @pl.when(t == pl.num_programs(1) - 1)
        def _():
            dn_out_ref[0] = dn_s[...]
            for p in range(RW_HEADS // 2):
                rw_out_ref[0, 2 * p] = rw_s[p][:D, :D]
                rw_out_ref[0, 2 * p + 1] = rw_s[p][D:, D:]
    else:
        for gi in range(NG):
            seqs = range(NS)
            rsl = lambda e, j: slice(e * GS + j * C, e * GS + (j + 1) * C)
            dn_items = [(j, h, gd_keys.index((gi, h // 2)), h % 2) for j in seqs for h in range(DN_HEADS)]
            rw_items = [(j, p, rw_keys.index((gi, p))) for j in seqs for p in range(RW_HEADS // 2)]
            s_dn = [dn0_ref[gi * NS + j, h] for j, h, _, _ in dn_items]
            s_rw = [_pair_blockdiag(rw0_ref[gi * NS + j, 2 * p], rw0_ref[gi * NS + j, 2 * p + 1])
                    for j, p, _ in rw_items]
            ss_dn = [_mm(_stack(w_dn[i][rsl(e, j)], qg2[i][rsl(e, j)]), s)
                     for s, (j, h, i, e) in zip(s_dn, dn_items)]
            ss_rw = [_mm_nt(_stack(tl[i][rsl(0, j)], tl[i][rsl(1, j)], ql[i][rsl(0, j)], ql[i][rsl(1, j)]), s)
                     for s, (j, p, i) in zip(s_rw, rw_items)]
            v_new = [u_dn[i][rsl(e, j)] - ss[:C] for ss, (j, h, i, e) in zip(ss_dn, dn_items)]
            o_seq = [ss[C:] + o2_dn[i][rsl(e, j)] for ss, (j, h, i, e) in zip(ss_dn, dn_items)]
            us = [ss[:2 * C] + _stack(w2[i][rsl(0, j)], w2[i][rsl(1, j)]) for ss, (j, p, i) in zip(ss_rw, rw_items)]
            ysq = [ss[2 * C:] + _stack(y2[i][rsl(0, j)], y2[i][rsl(1, j)]) for ss, (j, p, i) in zip(ss_rw, rw_items)]
            u1 = [x[:C] + x[C:] for x in us]
            y_seq = [x[:C] + x[C:] for x in ysq]
            for s, vn, (j, h, i, e) in zip(s_dn, v_new, dn_items):
                row = e * GS + j * C
                dn_out_ref[gi * NS + j, h] = (s * e_gl[i][row:row + 1, :]
                                             + _mm_tn(k_dec[i][rsl(e, j)], vn))
            for s, x, (j, p, i) in zip(s_rw, u1, rw_items):
                sq = slice(j * C, (j + 1) * C)
                s_new = ((s + _mm_tn(_stack(v_p[i][sq], -x), _stack(kt_p[i][sq], a_t[i][sq])))
                         * bd * p_last[gi][j * C:j * C + 1, pl_(p)])
                rw_out_ref[gi * NS + j, 2 * p] = s_new[:D, :D]
                rw_out_ref[gi * NS + j, 2 * p + 1] = s_new[D:, D:]
            for h in range(DN_HEADS):
                parts = [o for o, (j, hh, _, _) in zip(o_seq, dn_items) if hh == h]
                o_rows[h][gi] = parts[0] if NS == 1 else jnp.concatenate(parts, axis=0)
            for p in range(RW_HEADS // 2):
                parts = [y for y, (j, pp, _) in zip(y_seq, rw_items) if pp == p]
                y_rows[p][gi] = parts[0] if NS == 1 else jnp.concatenate(parts, axis=0)

    cat = lambda parts: parts[0] if len(parts) == 1 else jnp.concatenate(parts, axis=0)
    dn_w = dnnorm_ref[...]
    for h in range(DN_HEADS):
        zh = z[:, h * LANES:(h + 1) * LANES]
        put(slice(h * LANES, (h + 1) * LANES), _rms(cat(o_rows[h]), dn_w) * (zh * _sigmoid(zh)))
    for p in range(RW_HEADS // 2):
        sl = pl_(p)
        y = cat(y_rows[p])
        mu_g = gsum(y) * (1.0 / RW_HEAD_DIM)
        yc = y - mu_g
        var = gsum(yc * yc) * (1.0 / RW_HEAD_DIM)
        yn = yc * lax.rsqrt(var + GN_EPS) * ln_w[:, sl] + ln_b[:, sl]
        bonus = gsum(bonus_in[:, sl]) * vr[:, sl]
        put(slice(DN_WIDTH + p * LANES, DN_WIDTH + (p + 1) * LANES), (yn + bonus) * g[:, sl])


def _mixer(proj, dn0, rw0, lp, *, batch, seq, rows, group, chunk, n_pad, carry, layer=None, hist=(),
           acc=()):
    full = lambda shape: pl.BlockSpec(shape, lambda b, t: (0,) * len(shape))
    dn_blk = (DN_HEADS, DN_HEAD_DIM, DN_HEAD_DIM)
    rw_blk = (RW_HEADS, RW_HEAD_DIM, RW_HEAD_DIM)
    if carry:
        nt = seq // rows
        grid = (batch, nt)
        bsel = (lambda b: b) if dn0.shape[0] == batch else (lambda b: 0)
        row_map = lambda b, t: (b * nt + t, 0)
        st_in = [pl.BlockSpec((1,) + blk, lambda b, t: (bsel(b), 0, 0, 0)) for blk in (dn_blk, rw_blk)]
        st_out = [pl.BlockSpec((1,) + blk, lambda b, t: (b, 0, 0, 0)) for blk in (dn_blk, rw_blk)]
        scratch = [pltpu.VMEM((DN_HEADS, LANES, LANES), F32), pltpu.VMEM((RW_HEADS // 2, LANES, LANES), F32)]
        io_rows = rows
    else:
        nseq = rows // chunk
        grid = (batch // nseq, 1)
        row_map = lambda b, t: (b, 0)
        io_rows = nseq * (chunk - n_pad)
        st_in = [pl.BlockSpec((None, nseq) + blk, lambda b, t: (layer, b, 0, 0, 0)) for blk in (dn_blk, rw_blk)]
        st_in += [pl.BlockSpec(memory_space=pl.ANY)] * len(acc)
        st_out = [pl.BlockSpec((None, nseq) + blk, lambda b, t: (layer, b, 0, 0, 0)) for blk in (dn_blk, rw_blk)]
        scratch = []
    state_shapes = ([jax.ShapeDtypeStruct(a.shape, F32) for a in acc] if acc else
                    [jax.ShapeDtypeStruct((batch,) + blk, F32) for blk in (dn_blk, rw_blk)])
    first_acc = 1 + len(hist) + 2
    return pl.pallas_call(
        functools.partial(_mixer_body, rows=rows, group=group, chunk=chunk, n_pad=n_pad, carry=carry),
        grid=grid,
        in_specs=[
            pl.BlockSpec((io_rows, P_COLS), row_map),
            *[pl.BlockSpec(h.shape[:-2] + (rows // chunk, h.shape[-1]), lambda b, t, n=h.ndim: (0,) * (n - 2) + (b, 0))
              for h in hist],
            *st_in,
            full((SUBLANES, DN_QKV)), full((SUBLANES, LANES)), full((1, LANES)),
            full((1, RW_COLS)), full((SUBLANES, RW_WIDTH)),
            full((LANES, RW_WIDTH)), full((LANES, RW_WIDTH)), full((G_LORA, RW_WIDTH)),
        ],
        out_specs=[pl.BlockSpec((io_rows, D_MODEL), row_map), *st_out],
        out_shape=[jax.ShapeDtypeStruct((proj.shape[0], D_MODEL), F32), *state_shapes],
        input_output_aliases={first_acc + i: 1 + i for i in range(len(acc))},
        scratch_shapes=scratch,
        compiler_params=pltpu.CompilerParams(
            dimension_semantics=("parallel", "arbitrary"), vmem_limit_bytes=VMEM_LIMIT),
        name="mixer",
    )(proj, *hist, dn0, rw0, *acc, lp["conv_w"], lp["gdnp"], lp["dn_norm"], lp["mu"],
      lp["rwvec"], lp["w_up"], lp["a_up"], lp["g_up"])


def _prep_layer_params(l, w_in, dn_conv_w, dn_a_log, dn_dt_bias, dn_norm_w, rw_mu, rw_w0, rw_w_up,
                       rw_a0, rw_a_up, rw_g_up, rw_k_k, rw_k_a, rw_r_k, rw_ln_w, rw_ln_b):
    off = DN_QKV + DN_WIDTH
    wi = w_in[l].astype(BF16)
    w_in_p = jnp.concatenate(
        [wi[:, :off + 2 * DN_HEADS], jnp.zeros((D_MODEL, LANES - 2 * DN_HEADS), BF16),
         wi[:, off + 2 * DN_HEADS:]], axis=1)
    gdnp = jnp.zeros((SUBLANES, LANES), F32)
    gdnp = gdnp.at[0, DN_HEADS:2 * DN_HEADS].set(dn_dt_bias[l])
    gdnp = gdnp.at[1, DN_HEADS:2 * DN_HEADS].set(dn_a_log[l])
    rwvec = jnp.stack([rw_w0[l], rw_a0[l], rw_k_k[l], rw_k_a[l], rw_r_k[l].reshape(RW_WIDTH),
                       rw_ln_w[l], rw_ln_b[l], jnp.zeros((RW_WIDTH,), F32)])
    zeros_lora = jnp.zeros((W_LORA, RW_WIDTH), F32)
    return {
        "w_in": w_in_p,
        "conv_w": jnp.concatenate([dn_conv_w[l], jnp.zeros((SUBLANES - CONV_W, DN_QKV), F32)], axis=0),
        "gdnp": gdnp,
        "dn_norm": dn_norm_w[l].reshape(1, LANES),
        "mu": rw_mu[l].reshape(1, RW_COLS),
        "rwvec": rwvec,
        "w_up": jnp.concatenate([rw_w_up[l], zeros_lora], axis=0),
        "a_up": jnp.concatenate([zeros_lora, rw_a_up[l]], axis=0),
        "g_up": rw_g_up[l],
    }


def _post_layer(x, mixed, ffn, l):
    return _post(x, mixed, ffn["w_out"][l], ffn["g_post_mix"][l], ffn["g_pre_ffn"][l],
                 ffn["w_gate"][l], ffn["w_up"][l], ffn["w_down"][l], ffn["g_post_ffn"][l])


def _trunk_carry(x, states, layers, ffn, *, batch, seq, rows, group):
    new_states = []
    for l in range(DEPTH):
        conv_s, dn_s, shift_s, rw_s = states[l]
        prep, conv_t, shift_t = _inproj_prep(x, ffn["g_pre_mix"][l], layers[l]["w_in"], conv_s, shift_s,
                                             layers[l], batch=batch, seq=seq, rows=rows)
        mixed, dn_new, rw_new = _mixer(prep, dn_s, rw_s, layers[l], batch=batch, seq=seq, rows=rows,
                                       group=group, chunk=group, n_pad=0, carry=True)
        x = _post_layer(x, mixed, ffn, l)
        new_states.append((conv_t, dn_new, shift_t, rw_new))
    return x, new_states


def _trunk_sample(x, state_dn_conv, state_dn, state_rw_shift, state_rw, layers, ffn, *, batch, seq_real):
    seq = SUBLANES
    n_pad = seq - seq_real
    assert n_pad >= CONV_W - 1, "the masked front rows must hold the whole conv history"
    conv_new, shift_new = [], []
    dn_acc = lax.empty(state_dn.shape, F32)
    rw_acc = lax.empty(state_rw.shape, F32)
    for l in range(DEPTH):
        proj = _inproj(x, ffn["g_pre_mix"][l], layers[l]["w_in"])
        hist = (jnp.transpose(state_dn_conv[l], (1, 0, 2)), state_rw_shift[l])
        mixed, dn_acc, rw_acc = _mixer(proj, state_dn, state_rw, layers[l], batch=batch, seq=seq,
                                       rows=SAMPLE_ROWS, group=GROUP, chunk=seq, n_pad=n_pad, carry=False,
                                       layer=l, hist=hist, acc=(dn_acc, rw_acc))
        x = _post_layer(x, mixed, ffn, l)
        conv_new.append(proj[:, C_QKV:C_QKV + DN_QKV].reshape(batch, seq_real, DN_QKV)[:, 1 - CONV_W:])
        shift_new.append(proj[:, C_RW:].reshape(batch, seq_real, RW_COLS)[:, -1])
    return x, (jnp.stack(conv_new), dn_acc, jnp.stack(shift_new), rw_acc)


def kernel(x_prompt, x_sample, state_dn_conv, state_dn, state_rw_shift, state_rw, meta, g_pre_mix, g_post_mix, g_pre_ffn, g_post_ffn, w_in, dn_conv_w, dn_a_log, dn_dt_bias, dn_norm_w, rw_mu, rw_w0, rw_w_up, rw_a0, rw_a_up, rw_g_up, rw_k_k, rw_k_a, rw_r_k, rw_ln_w, rw_ln_b, w_out, w_gate, w_up, w_down):
    batch, seq, _ = x_prompt.shape
    dec_batch, dec_seq, _ = x_sample.shape
    layers = [_prep_layer_params(l, w_in, dn_conv_w, dn_a_log, dn_dt_bias, dn_norm_w, rw_mu, rw_w0,
                                 rw_w_up, rw_a0, rw_a_up, rw_g_up, rw_k_k, rw_k_a, rw_r_k, rw_ln_w,
                                 rw_ln_b) for l in range(DEPTH)]
    vec = lambda g: g.reshape(DEPTH, 1, D_MODEL)
    ffn = {"g_pre_mix": vec(g_pre_mix), "g_post_mix": vec(g_post_mix), "g_pre_ffn": vec(g_pre_ffn),
           "g_post_ffn": vec(g_post_ffn), "w_out": w_out.astype(BF16), "w_gate": w_gate.astype(BF16),
           "w_up": w_up.astype(BF16), "w_down": w_down.astype(BF16)}

    zero_states = [(jnp.zeros((1, SUBLANES, DN_QKV), F32), jnp.zeros((1, DN_HEADS, DN_HEAD_DIM, DN_HEAD_DIM), F32),
                    jnp.zeros((1, SUBLANES, RW_COLS), F32),
                    jnp.zeros((1, RW_HEADS, RW_HEAD_DIM, RW_HEAD_DIM), F32))] * DEPTH
    _, meta_states = _trunk_carry(meta.astype(F32), zero_states, layers, ffn,
                                  batch=1, seq=N_META, rows=N_META, group=N_META)

    yp, ps = _trunk_carry(x_prompt.reshape(batch * seq, D_MODEL), meta_states, layers, ffn,
                          batch=batch, seq=seq, rows=PROMPT_ROWS, group=GROUP)
    ps = [(c[:, SUBLANES - (CONV_W - 1):], d, s[:, SUBLANES - 1], r) for c, d, s, r in ps]

    ys, (cs, ds, sh, rs) = _trunk_sample(x_sample.reshape(dec_batch * dec_seq, D_MODEL), state_dn_conv,
                                         state_dn, state_rw_shift, state_rw, layers, ffn,
                                         batch=dec_batch, seq_real=dec_seq)

    cp, dp, sp, rp = (jnp.stack([s[i] for s in ps]) for i in range(4))
    return (yp.reshape(batch, seq, D_MODEL), ys.reshape(dec_batch, dec_seq, D_MODEL),
            cp, dp, sp, rp, cs, ds, sh, rs)
```
